```python
import jax, jax.numpy as jnp
from jax import lax
import numpy as np

D_MODEL = 4096
BATCH = 2
SEQ = 8192
DEPTH = 1

MLA_HEADS = 16
QK_NOPE_DIM = 128
QK_ROPE_DIM = 64
V_HEAD_DIM = 128
Q_LORA_RANK = 768
KV_LORA_RANK = 512
MLA_WIDTH = MLA_HEADS * V_HEAD_DIM
ROPE_THETA = 10000.0
Q_BLOCK = 128

HG_HEADS = 16
HG_KEY_DIM = 128
HG_VAL_DIM = 128
HG_FORGET_WIDTH = HG_HEADS * HG_KEY_DIM
HG_WIDTH = HG_HEADS * HG_VAL_DIM
HG_CHUNK = 64

MIX_WIDTH = MLA_WIDTH + HG_WIDTH
IN_SPLITS = (Q_LORA_RANK, KV_LORA_RANK, QK_ROPE_DIM,
             HG_FORGET_WIDTH, HG_FORGET_WIDTH, HG_WIDTH, HG_WIDTH)
N_IN = Q_LORA_RANK + KV_LORA_RANK + QK_ROPE_DIM + 2 * HG_FORGET_WIDTH + 2 * HG_WIDTH

D_FF = 11008
CONV_WIDTH = 3

EPS = 1e-6

kernel_name = "hybrid_mla_hgrn2_convglu_block"


def rms_norm(x, g):
    xf = x.astype(jnp.float32)
    y = xf * lax.rsqrt(jnp.mean(xf * xf, axis=-1, keepdims=True) + EPS)
    return (y * g.astype(jnp.float32)).astype(x.dtype)


def rope_cos_sin(positions):
    half = QK_ROPE_DIM // 2
    inv_freq = 1.0 / (ROPE_THETA ** (jnp.arange(half, dtype=jnp.float32) * 2.0 / QK_ROPE_DIM))
    ang = positions.astype(jnp.float32)[..., None] * inv_freq
    return jnp.cos(ang), jnp.sin(ang)


def apply_rope(x, cos, sin):
    half = QK_ROPE_DIM // 2
    xf = x.astype(jnp.float32)
    x1, x2 = xf[..., :half], xf[..., half:]
    return jnp.concatenate([x1 * cos - x2 * sin, x2 * cos + x1 * sin], axis=-1).astype(x.dtype)


def causal_block_attention(q_nope, q_rope, k_nope, k_rope, v):
    B, S, H, _ = q_nope.shape
    n_blocks = S // Q_BLOCK
    scale = (QK_NOPE_DIM + QK_ROPE_DIM) ** -0.5
    key_pos = jnp.arange(S)

    def block(i):
        start = i * Q_BLOCK
        qn = lax.dynamic_slice_in_dim(q_nope, start, Q_BLOCK, axis=1)
        qr = lax.dynamic_slice_in_dim(q_rope, start, Q_BLOCK, axis=1)
        s = (jnp.einsum('bqhd,bkhd->bhqk', qn, k_nope)
             + jnp.einsum('bqhr,bkr->bhqk', qr, k_rope)).astype(jnp.float32) * scale
        q_pos = start + jnp.arange(Q_BLOCK)
        mask = key_pos[None, :] <= q_pos[:, None]
        s = jnp.where(mask[None, None], s, -jnp.inf)
        p = jax.nn.softmax(s, axis=-1).astype(v.dtype)
        return jnp.einsum('bhqk,bkhd->bqhd', p, v)

    out = lax.map(block, jnp.arange(n_blocks))
    return out.transpose(1, 0, 2, 3, 4).reshape(B, S, H, V_HEAD_DIM)


def mla_mixer(c_q, c_kv, k_rope, cos, sin, q_norm_g, w_uq, kv_norm_g, w_ukv, out_norm_g):
    B, S, _ = c_q.shape
    q = (rms_norm(c_q, q_norm_g) @ w_uq).reshape(B, S, MLA_HEADS, QK_NOPE_DIM + QK_ROPE_DIM)
    q_nope, q_rope = q[..., :QK_NOPE_DIM], q[..., QK_NOPE_DIM:]
    q_rope = apply_rope(q_rope, cos[:, :, None, :], sin[:, :, None, :])
    k_rope = apply_rope(k_rope, cos, sin)
    kv = (rms_norm(c_kv, kv_norm_g) @ w_ukv).reshape(B, S, MLA_HEADS, QK_NOPE_DIM + V_HEAD_DIM)
    k_nope, v = kv[..., :QK_NOPE_DIM], kv[..., QK_NOPE_DIM:]
    o = causal_block_attention(q_nope, q_rope, k_nope, k_rope, v)
    o = rms_norm(o, out_norm_g.reshape(MLA_HEADS, V_HEAD_DIM))
    return o.reshape(B, S, MLA_WIDTH)


def hgrn2_mixer(hq, hf, hi, hgate, lb, norm_g):
    B, S, _ = hq.shape
    f32 = jnp.float32
    lbf = lb.astype(f32)
    log_f = jnp.logaddexp(jnp.log(lbf), jnp.log1p(-lbf) + jax.nn.log_sigmoid(hf.astype(f32)))
    k = -jnp.expm1(log_f)
    q = hq.astype(f32).reshape(B, S, HG_HEADS, HG_KEY_DIM) * (HG_KEY_DIM ** -0.5)
    k = k.reshape(B, S, HG_HEADS, HG_KEY_DIM)
    g = log_f.reshape(B, S, HG_HEADS, HG_KEY_DIM)
    v = hi.astype(f32).reshape(B, S, HG_HEADS, HG_VAL_DIM)
    n_chunks = S // HG_CHUNK

    def to_chunks(t):
        return t.reshape(B, n_chunks, HG_CHUNK, HG_HEADS, t.shape[-1]).transpose(1, 0, 3, 2, 4)

    tri = jnp.tril(jnp.ones((HG_CHUNK, HG_CHUNK), dtype=bool))

    def step(state, xs):
        qc, kc, vc, gc = xs
        G = jnp.cumsum(gc, axis=2)
        inter = jnp.einsum('bhtd,bhde->bhte', qc * jnp.exp(G), state)
        diff = G[:, :, :, None, :] - G[:, :, None, :, :]
        decay = jnp.exp(jnp.where(tri[:, :, None], diff, -jnp.inf))
        A = jnp.einsum('bhtd,bhsd,bhtsd->bhts', qc, kc, decay)
        o = inter + jnp.einsum('bhts,bhse->bhte', A, vc)
        G_last = G[:, :, -1:, :]
        state = (jnp.exp(G_last[:, :, 0, :])[..., None] * state
                 + jnp.einsum('bhsd,bhse->bhde', kc * jnp.exp(G_last - G), vc))
        return state, o

    s0 = jnp.zeros((B, HG_HEADS, HG_KEY_DIM, HG_VAL_DIM), f32)
    _, o = lax.scan(step, s0, (to_chunks(q), to_chunks(k), to_chunks(v), to_chunks(g)))
    o = o.transpose(1, 0, 3, 2, 4).reshape(B, S, HG_HEADS, HG_VAL_DIM)
    o = rms_norm(o, norm_g.reshape(HG_HEADS, HG_VAL_DIM)).reshape(B, S, HG_WIDTH)
    return (o * jax.nn.silu(hgate.astype(f32))).astype(hq.dtype)


def conv_glu_ffn(x, w_up, conv_w, conv_b, w_down):
    S = x.shape[1]
    gu = x @ w_up
    gate, up = gu[..., :D_FF], gu[..., D_FF:]
    gp = jnp.pad(gate, ((0, 0), (CONV_WIDTH - 1, 0), (0, 0)))
    conv = conv_b
    for tap in range(CONV_WIDTH):
        conv = conv + conv_w[tap] * gp[:, tap:tap + S]
    return (jax.nn.silu(conv) * up) @ w_down


def setup_inputs(seed: int = 0) -> dict:
    key = jax.random.key(seed)
    ks = jax.random.split(key, 18)
    f32 = jnp.float32

    def nrm(k, shape, scale):
        return jax.random.normal(k, shape, f32) * scale

    def gain(k, shape):
        return 1.0 + 0.02 * jax.random.normal(k, shape, f32)

    return {
        "x": nrm(ks[0], (BATCH, SEQ, D_MODEL), 1.0),
        "positions": jnp.broadcast_to(jnp.arange(SEQ, dtype=jnp.int32)[None, :], (BATCH, SEQ)),
        "mix_norm_g": gain(ks[1], (DEPTH, D_MODEL)),
        "w_in": nrm(ks[2], (DEPTH, D_MODEL, N_IN), D_MODEL ** -0.5),
        "q_norm_g": gain(ks[3], (DEPTH, Q_LORA_RANK)),
        "w_uq": nrm(ks[4], (DEPTH, Q_LORA_RANK, MLA_HEADS * (QK_NOPE_DIM + QK_ROPE_DIM)), Q_LORA_RANK ** -0.5),
        "kv_norm_g": gain(ks[5], (DEPTH, KV_LORA_RANK)),
        "w_ukv": nrm(ks[6], (DEPTH, KV_LORA_RANK, MLA_HEADS * (QK_NOPE_DIM + V_HEAD_DIM)), KV_LORA_RANK ** -0.5),
        "attn_out_norm_g": gain(ks[7], (DEPTH, MLA_WIDTH)),
        "lb_logits": nrm(ks[8], (DEPTH + 1, HG_FORGET_WIDTH), 0.1),
        "hg_norm_g": gain(ks[9], (DEPTH, HG_WIDTH)),
        "w_out": nrm(ks[10], (DEPTH, MIX_WIDTH, D_MODEL), MIX_WIDTH ** -0.5),
        "ffn_norm_g": gain(ks[11], (DEPTH, D_MODEL)),
        "w_up": nrm(ks[12], (DEPTH, D_MODEL, 2 * D_FF), D_MODEL ** -0.5),
        "conv_w": nrm(ks[13], (DEPTH, CONV_WIDTH, D_FF), CONV_WIDTH ** -0.5),
        "conv_b": nrm(ks[14], (DEPTH, D_FF), 0.01),
        "w_down": nrm(ks[15], (DEPTH, D_FF, D_MODEL), D_FF ** -0.5),
        "final_norm_g": gain(ks[16], (D_MODEL,)),
    }


def reference(x, positions, mix_norm_g, w_in, q_norm_g, w_uq, kv_norm_g, w_ukv,
              attn_out_norm_g, lb_logits, hg_norm_g, w_out, ffn_norm_g, w_up,
              conv_w, conv_b, w_down, final_norm_g):
    cos, sin = rope_cos_sin(positions)
    lb_table = jnp.cumsum(jax.nn.softmax(lb_logits.astype(jnp.float32), axis=0), axis=0)
    offsets = [sum(IN_SPLITS[:i + 1]) for i in range(len(IN_SPLITS) - 1)]
    h = x
    for l in range(DEPTH):
        hn = rms_norm(h, mix_norm_g[l])
        proj = hn @ w_in[l]
        c_q, c_kv, k_rope, hq, hf, hi, hgate = jnp.split(proj, offsets, axis=-1)
        mla_out = mla_mixer(c_q, c_kv, k_rope, cos, sin, q_norm_g[l], w_uq[l],
                            kv_norm_g[l], w_ukv[l], attn_out_norm_g[l])
        hg_out = hgrn2_mixer(hq, hf, hi, hgate, lb_table[l], hg_norm_g[l])
        h = h + jnp.concatenate([mla_out, hg_out], axis=-1) @ w_out[l]
        h = h + conv_glu_ffn(rms_norm(h, ffn_norm_g[l]), w_up[l], conv_w[l], conv_b[l], w_down[l])
    return rms_norm(h, final_norm_g)
```

```python
import functools
import math

import jax
import jax.numpy as jnp
from jax import lax
from jax.experimental import pallas as pl
from jax.experimental.pallas import tpu as pltpu

F32 = jnp.float32
BF16 = jnp.bfloat16

MLA_HEADS = 16
QK_NOPE = 128
QK_ROPE = 64
V_DIM = 128
Q_LORA = 768
KV_LORA = 512
ROPE_THETA = 10000.0
HG_HEADS = 16
HG_DIM = 128
HG_CHUNK = 64
HG_SUB = 16
CONV_WIDTH = 3
EPS = 1e-6

LANE = 128
SUBLANE = 8
VMEM_LIMIT_BYTES = 56 * 1024 * 1024

LATENT_PAD = 1536
Q_HEAD_W = 2 * LANE

_NT = (((1,), (1,)), ((), ()))
_TN = (((0,), (0,)), ((), ()))


def _params(*sem):
    return pltpu.CompilerParams(dimension_semantics=sem, vmem_limit_bytes=VMEM_LIMIT_BYTES)


def _tile(n, pref):
    t = min(n, pref)
    assert n % t == 0, (n, t)
    return t


def _rms_kernel(x_ref, g_ref, o_ref):
    x = x_ref[...]
    ms = jnp.mean(x * x, axis=-1, keepdims=True)
    o_ref[...] = (x * lax.rsqrt(ms + EPS) * g_ref[...]).astype(o_ref.dtype)


def _rmsnorm(x, g, out_dtype, name):
    m, d = x.shape
    tm = _tile(m, 256)
    return pl.pallas_call(
        _rms_kernel,
        out_shape=jax.ShapeDtypeStruct((m, d), out_dtype),
        grid=(m // tm,),
        in_specs=[pl.BlockSpec((tm, d), lambda i: (i, 0)),
                  pl.BlockSpec((1, d), lambda i: (0, 0))],
        out_specs=pl.BlockSpec((tm, d), lambda i: (i, 0)),
        compiler_params=_params("parallel"),
        name=name,
    )(x, g.reshape(1, d))


def _mm_kernel(a_ref, b_ref, o_ref):
    o_ref[...] = jnp.dot(a_ref[...], b_ref[...], preferred_element_type=F32).astype(o_ref.dtype)


def _matmul(a, b, tm, tn, name):
    m, k = a.shape
    _, n = b.shape
    tm, tn = _tile(m, tm), _tile(n, tn)
    return pl.pallas_call(
        _mm_kernel,
        out_shape=jax.ShapeDtypeStruct((m, n), F32),
        grid=(m // tm, n // tn),
        in_specs=[pl.BlockSpec((tm, k), lambda i, j: (i, 0)),
                  pl.BlockSpec((k, tn), lambda i, j: (0, j))],
        out_specs=pl.BlockSpec((tm, tn), lambda i, j: (i, j)),
        compiler_params=_params("parallel", "parallel"),
        name=name,
    )(a, b)


def _latent_kernel(c_ref, pos_ref, invf_ref, qg_ref, kvg_ref, wq_ref, wkv_ref,
                   q_ref, kv_ref, kr_ref):
    c = c_ref[...]
    ang = pos_ref[...].astype(F32) * invf_ref[...]
    lane = lax.broadcasted_iota(jnp.int32, ang.shape, 1)
    sin = jnp.sin(ang)
    cs = jnp.where(lane < QK_ROPE, jnp.cos(ang),
                   jnp.where(lane < QK_ROPE + QK_ROPE // 2, -sin, sin))

    cq = c[:, :Q_LORA]
    cqn = (cq * lax.rsqrt(jnp.mean(cq * cq, axis=-1, keepdims=True) + EPS)
           * qg_ref[...]).astype(BF16)
    ckv = c[:, Q_LORA:Q_LORA + KV_LORA]
    ckvn = (ckv * lax.rsqrt(jnp.mean(ckv * ckv, axis=-1, keepdims=True) + EPS)
            * kvg_ref[...]).astype(BF16)

    for h in range(MLA_HEADS):
        qh = jnp.dot(cqn, wq_ref[:, h * Q_HEAD_W:(h + 1) * Q_HEAD_W],
                     preferred_element_type=F32)
        q_ref[:, h * Q_HEAD_W:h * Q_HEAD_W + QK_NOPE] = qh[:, :QK_NOPE].astype(BF16)
        q_ref[:, h * Q_HEAD_W + QK_NOPE:(h + 1) * Q_HEAD_W] = (qh[:, QK_NOPE:] * cs).astype(BF16)

    kv_ref[...] = jnp.dot(ckvn, wkv_ref[...], preferred_element_type=F32).astype(BF16)

    t = c[:, Q_LORA + KV_LORA:Q_LORA + KV_LORA + LANE] * cs
    kr_ref[...] = (t + pltpu.roll(t, QK_ROPE, axis=1)).astype(BF16)


def _latent_up(proj, pos128, invf, qg, kvg, wq_ext, wkv):
    m = proj.shape[0]
    ts = _tile(m, 512)
    nq = wq_ext.shape[1]
    nkv = wkv.shape[1]
    return pl.pallas_call(
        _latent_kernel,
        out_shape=(jax.ShapeDtypeStruct((m, nq), BF16),
                   jax.ShapeDtypeStruct((m, nkv), BF16),
                   jax.ShapeDtypeStruct((m, LANE), BF16)),
        grid=(m // ts,),
        in_specs=[pl.BlockSpec((ts, LATENT_PAD), lambda i: (i, 0)),
                  pl.BlockSpec((ts, LANE), lambda i: (i, 0)),
                  pl.BlockSpec((1, LANE), lambda i: (0, 0)),
                  pl.BlockSpec((1, Q_LORA), lambda i: (0, 0)),
                  pl.BlockSpec((1, KV_LORA), lambda i: (0, 0)),
                  pl.BlockSpec((Q_LORA, nq), lambda i: (0, 0)),
                  pl.BlockSpec((KV_LORA, nkv), lambda i: (0, 0))],
        out_specs=(pl.BlockSpec((ts, nq), lambda i: (i, 0)),
                   pl.BlockSpec((ts, nkv), lambda i: (i, 0)),
                   pl.BlockSpec((ts, LANE), lambda i: (i, 0))),
        compiler_params=_params("parallel"),
        name="latent_up",
    )(proj, pos128, invf, qg, kvg, wq_ext, wkv)


def _attn_kernel(q_ref, kn_ref, v_ref, kr_ref, g_ref, o_ref, kfull_ref, *, tq, scale):
    qi = pl.program_id(2)

    @pl.when(qi == 0)
    def _():
        kfull_ref[:, :QK_NOPE] = kn_ref[...]
        kfull_ref[:, QK_NOPE:] = kr_ref[...]

    q = q_ref[...]
    c = scale * math.log2(math.e)

    def step(kb, carry, masked):
        m, l, acc = carry
        start = pl.multiple_of(kb * tq, tq)
        k = kfull_ref[pl.ds(start, tq), :]
        v = v_ref[pl.ds(start, tq), :]
        s = lax.dot_general(q, k, _NT, preferred_element_type=F32)
        if masked:
            row = lax.broadcasted_iota(jnp.int32, s.shape, 0)
            col = lax.broadcasted_iota(jnp.int32, s.shape, 1)
            s = jnp.where(col <= row, s, -jnp.inf)
        m_new = jnp.maximum(m, jnp.max(s, axis=-1, keepdims=True))
        p = jnp.exp2((s - m_new) * c)
        alpha = jnp.exp2((m - m_new) * c)
        l = alpha * l + jnp.sum(p, axis=-1, keepdims=True)
        acc = alpha * acc + jnp.dot(p.astype(BF16), v, preferred_element_type=F32)
        return m_new, l, acc

    init = (jnp.full((tq, 1), -jnp.inf, F32), jnp.zeros((tq, 1), F32),
            jnp.zeros((tq, V_DIM), F32))
    carry = lax.fori_loop(0, qi, lambda kb, cr: step(kb, cr, False), init)
    _, l, acc = step(qi, carry, True)

    o = acc / l
    ms = jnp.mean(o * o, axis=-1, keepdims=True)
    o_ref[...] = (o * lax.rsqrt(ms + EPS) * g_ref[...]).astype(o_ref.dtype)


def _attention(q, kv, kr, g, batch, seq):
    m = q.shape[0]
    tq = _tile(seq, 512)
    nq = seq // tq
    scale = (QK_NOPE + QK_ROPE) ** -0.5
    return pl.pallas_call(
        functools.partial(_attn_kernel, tq=tq, scale=scale),
        out_shape=jax.ShapeDtypeStruct((m, MLA_HEADS * V_DIM), BF16),
        grid=(batch, MLA_HEADS, nq),
        in_specs=[pl.BlockSpec((tq, Q_HEAD_W), lambda b, h, i: (b * nq + i, h)),
                  pl.BlockSpec((seq, QK_NOPE), lambda b, h, i: (b, 2 * h)),
                  pl.BlockSpec((seq, V_DIM), lambda b, h, i: (b, 2 * h + 1)),
                  pl.BlockSpec((seq, LANE), lambda b, h, i: (b, 0)),
                  pl.BlockSpec((1, V_DIM), lambda b, h, i: (0, h))],
        out_specs=pl.BlockSpec((tq, V_DIM), lambda b, h, i: (b * nq + i, h)),
        scratch_shapes=[pltpu.VMEM((seq, Q_HEAD_W), BF16)],
        compiler_params=_params("parallel", "parallel", "arbitrary"),
        name="mla_attention",
    )(q, kv, kv, kr, g)


def _hgrn_kernel(hq_ref, hf_ref, hi_ref, hg_ref, lb_ref, g_ref, o_ref, st_ref, a_ref, *, ts):
    si = pl.program_id(2)

    @pl.when(si == 0)
    def _():
        st_ref[...] = jnp.zeros_like(st_ref)

    l0 = lb_ref[0:1, :]
    l1 = lb_ref[1:2, :]
    lmax = jnp.maximum(l0, l1)
    e0 = jnp.exp(l0 - lmax)
    e1 = jnp.exp(l1 - lmax)
    lb = e0 / (e0 + e1)
    log_lb = jnp.log(lb)
    log1m_lb = jnp.log1p(-lb)
    gn = g_ref[...]
    q_scale = HG_DIM ** -0.5

    tri = (lax.broadcasted_iota(jnp.int32, (HG_CHUNK, HG_CHUNK), 0)
           >= lax.broadcasted_iota(jnp.int32, (HG_CHUNK, HG_CHUNK), 1))
    tri = jnp.where(tri, 1.0, 0.0).astype(BF16)
    sub_row = lax.broadcasted_iota(jnp.int32, (HG_SUB, HG_SUB), 0)
    sub_col = lax.broadcasted_iota(jnp.int32, (HG_SUB, HG_SUB), 1)

    def chunk_body(ci, st):
        r0 = pl.multiple_of(ci * HG_CHUNK, HG_CHUNK)
        rows = pl.ds(r0, HG_CHUNK)
        z = hf_ref[rows, :]
        sp = jnp.log1p(jnp.exp(-jnp.abs(z)))
        log_sig = jnp.minimum(z, 0.0) - sp
        log_sig_neg = jnp.minimum(-z, 0.0) - sp
        b = log1m_lb + log_sig
        log_f = jnp.maximum(log_lb, b) + jnp.log1p(jnp.exp(-jnp.abs(log_lb - b)))
        kk = jnp.exp(log1m_lb + log_sig_neg)

        g1 = log_f.astype(BF16)
        r1 = log_f - g1.astype(F32)
        g2 = r1.astype(BF16)
        g3 = (r1 - g2.astype(F32)).astype(BF16)
        gs = jnp.dot(tri, jnp.concatenate([g1, g2, g3], axis=1), preferred_element_type=F32)
        G = gs[:, :LANE] + gs[:, LANE:2 * LANE] + gs[:, 2 * LANE:]

        qs = hq_ref[rows, :] * q_scale
        vb = hi_ref[rows, :].astype(BF16)

        inter = lax.dot_general((qs * jnp.exp(G)).astype(BF16), st.astype(BF16), _NT,
                                preferred_element_type=F32)

        a_ref[...] = jnp.zeros_like(a_ref)
        for i in range(HG_CHUNK // HG_SUB):
            lo = i * HG_SUB
            Gi = G[lo:lo + HG_SUB]
            qi = qs[lo:lo + HG_SUB]
            ki = kk[lo:lo + HG_SUB]
            if i > 0:
                bnd = G[lo - 1:lo]
                qh = (qi * jnp.exp(Gi - bnd)).astype(BF16)
                kt = (kk[:lo] * jnp.exp(bnd - G[:lo])).astype(BF16)
                a_ref[lo:lo + HG_SUB, 0:lo] = lax.dot_general(
                    qh, kt, _NT, preferred_element_type=F32)
            blk = jnp.zeros((HG_SUB, HG_SUB), F32)
            for s in range(HG_SUB):
                e = jnp.exp(jnp.minimum(Gi - Gi[s:s + 1], 0.0))
                col = jnp.sum(qi * ki[s:s + 1] * e, axis=-1, keepdims=True)
                blk = jnp.where(sub_col == s, col, blk)
            a_ref[lo:lo + HG_SUB, lo:lo + HG_SUB] = jnp.where(sub_row >= sub_col, blk, 0.0)

        o = inter + jnp.dot(a_ref[...].astype(BF16), vb, preferred_element_type=F32)

        g_last = G[HG_CHUNK - 1:HG_CHUNK]
        kdec = (kk * jnp.exp(g_last - G)).astype(BF16)
        st_new = st * jnp.exp(g_last) + lax.dot_general(vb, kdec, _TN,
                                                        preferred_element_type=F32)

        ms = jnp.mean(o * o, axis=-1, keepdims=True)
        y = o * lax.rsqrt(ms + EPS) * gn
        gate = hg_ref[rows, :]
        o_ref[rows, :] = (y * (gate * jax.nn.sigmoid(gate))).astype(o_ref.dtype)
        return st_new

    st_ref[...] = lax.fori_loop(0, ts // HG_CHUNK, chunk_body, st_ref[...])


def _hgrn(proj, lb_logits, g, batch, seq, col0):
    m = proj.shape[0]
    ts = _tile(seq, 512)
    ns = seq // ts
    hb = HG_HEADS
    c0 = col0 // LANE

    def spec(j):
        return pl.BlockSpec((ts, HG_DIM), lambda b, h, i: (b * ns + i, c0 + j * hb + h))

    return pl.pallas_call(
        functools.partial(_hgrn_kernel, ts=ts),
        out_shape=jax.ShapeDtypeStruct((m, HG_HEADS * HG_DIM), BF16),
        grid=(batch, HG_HEADS, ns),
        in_specs=[spec(0), spec(1), spec(2), spec(3),
                  pl.BlockSpec((2, HG_DIM), lambda b, h, i: (0, h)),
                  pl.BlockSpec((1, HG_DIM), lambda b, h, i: (0, h))],
        out_specs=pl.BlockSpec((ts, HG_DIM), lambda b, h, i: (b * ns + i, h)),
        scratch_shapes=[pltpu.VMEM((HG_DIM, HG_DIM), F32),
                        pltpu.VMEM((HG_CHUNK, HG_CHUNK), F32)],
        compiler_params=_params("parallel", "parallel", "arbitrary"),
        name="hgrn2",
    )(proj, proj, proj, proj, lb_logits, g)


def _outproj_kernel(a1_ref, a2_ref, w1_ref, w2_ref, r_ref, o_ref):
    acc = jnp.dot(a1_ref[...], w1_ref[...], preferred_element_type=F32)
    acc = acc + jnp.dot(a2_ref[...], w2_ref[...], preferred_element_type=F32)
    o_ref[...] = r_ref[...] + acc


def _outproj(a1, a2, w1, w2, res):
    m, k1 = a1.shape
    k2 = a2.shape[1]
    n = w1.shape[1]
    tm, tn = _tile(m, 1024), _tile(n, 512)
    return pl.pallas_call(
        _outproj_kernel,
        out_shape=jax.ShapeDtypeStruct((m, n), F32),
        grid=(m // tm, n // tn),
        in_specs=[pl.BlockSpec((tm, k1), lambda i, j: (i, 0)),
                  pl.BlockSpec((tm, k2), lambda i, j: (i, 0)),
                  pl.BlockSpec((k1, tn), lambda i, j: (0, j)),
                  pl.BlockSpec((k2, tn), lambda i, j: (0, j)),
                  pl.BlockSpec((tm, tn), lambda i, j: (i, j))],
        out_specs=pl.BlockSpec((tm, tn), lambda i, j: (i, j)),
        compiler_params=_params("parallel", "parallel"),
        name="out_proj",
    )(a1, a2, w1, w2, res)


def _upglu_kernel(x_ref, wg_ref, wu_ref, cw_ref, cb_ref, o_ref, halo_ref, *, tm, tiles_per_seq):
    mi = pl.program_id(0)
    n = pl.program_id(1)
    x = x_ref[...]
    gate = jnp.dot(x, wg_ref[...], preferred_element_type=F32)
    up = jnp.dot(x, wu_ref[...], preferred_element_type=F32)

    prev = halo_ref[n]
    prev = jnp.where(mi % tiles_per_seq == 0, 0.0, prev)
    p1 = prev[SUBLANE - 1:SUBLANE]
    p2 = prev[SUBLANE - 2:SUBLANE - 1]
    row = lax.broadcasted_iota(jnp.int32, gate.shape, 0)
    g1 = jnp.where(row == 0, p1, pltpu.roll(gate, 1, axis=0))
    g2 = jnp.where(row == 0, p2, jnp.where(row == 1, p1, pltpu.roll(gate, 2, axis=0)))
    cw = cw_ref[...]
    conv = cb_ref[...] + cw[0:1] * g2
    conv = conv + cw[1:2] * g1
    conv = conv + cw[2:3] * gate
    o_ref[...] = (conv * jax.nn.sigmoid(conv) * up).astype(o_ref.dtype)
    halo_ref[n] = gate[tm - SUBLANE:tm]


def _upglu(x, w_up, conv_w, conv_b, seq):
    m, k = x.shape
    dff = w_up.shape[1] // 2
    tm = _tile(seq, 1024)
    tn = 2 * LANE
    assert dff % tn == 0
    nn = dff // tn
    return pl.pallas_call(
        functools.partial(_upglu_kernel, tm=tm, tiles_per_seq=seq // tm),
        out_shape=jax.ShapeDtypeStruct((m, dff), BF16),
        grid=(m // tm, nn),
        in_specs=[pl.BlockSpec((tm, k), lambda i, j: (i, 0)),
                  pl.BlockSpec((k, tn), lambda i, j: (0, j)),
                  pl.BlockSpec((k, tn), lambda i, j: (0, nn + j)),
                  pl.BlockSpec((CONV_WIDTH, tn), lambda i, j: (0, j)),
                  pl.BlockSpec((1, tn), lambda i, j: (0, j))],
        out_specs=pl.BlockSpec((tm, tn), lambda i, j: (i, j)),
        scratch_shapes=[pltpu.VMEM((nn, SUBLANE, tn), F32)],
        compiler_params=_params("arbitrary", "arbitrary"),
        name="up_convglu",
    )(x, w_up, w_up, conv_w, conv_b)


def _down_kernel(a_ref, w_ref, r_ref, o_ref):
    o_ref[...] = r_ref[...] + jnp.dot(a_ref[...], w_ref[...], preferred_element_type=F32)


def _down(a, w, res):
    m, k = a.shape
    n = w.shape[1]
    tm, tn = _tile(m, 512), _tile(n, 256)
    return pl.pallas_call(
        _down_kernel,
        out_shape=jax.ShapeDtypeStruct((m, n), F32),
        grid=(m // tm, n // tn),
        in_specs=[pl.BlockSpec((tm, k), lambda i, j: (i, 0)),
                  pl.BlockSpec((k, tn), lambda i, j: (0, j)),
                  pl.BlockSpec((tm, tn), lambda i, j: (i, j))],
        out_specs=pl.BlockSpec((tm, tn), lambda i, j: (i, j)),
        compiler_params=_params("parallel", "parallel"),
        name="down_proj",
    )(a, w, res)


def kernel(x, positions, mix_norm_g, w_in, q_norm_g, w_uq, kv_norm_g, w_ukv, attn_out_norm_g,
           lb_logits, hg_norm_g, w_out, ffn_norm_g, w_up, conv_w, conv_b, w_down, final_norm_g):
    batch, seq, d = x.shape
    assert w_in.shape[0] == 1 and lb_logits.shape[0] == 2, "single-layer block expected"
    m = batch * seq
    x2 = x.reshape(m, d)
    half = QK_ROPE // 2

    w = w_in[0]
    o1, o2, o3 = Q_LORA, Q_LORA + KV_LORA, Q_LORA + KV_LORA + QK_ROPE
    wkr = w[:, o2:o3]
    wkr_sw = jnp.concatenate([wkr[:, half:], wkr[:, :half]], axis=1)
    w_in_b = jnp.concatenate(
        [w[:, :o2], wkr, wkr_sw, jnp.zeros((d, LATENT_PAD - o3 - QK_ROPE), w.dtype), w[:, o3:]],
        axis=1).astype(BF16)

    wq = w_uq[0].reshape(Q_LORA, MLA_HEADS, QK_NOPE + QK_ROPE)
    wq_rope = wq[..., QK_NOPE:]
    wq_ext = jnp.concatenate(
        [wq, wq_rope[..., half:], wq_rope[..., :half]], axis=-1
    ).reshape(Q_LORA, MLA_HEADS * Q_HEAD_W).astype(BF16)
    wkv = w_ukv[0].astype(BF16)
    mla_w = MLA_HEADS * V_DIM
    w_out1 = w_out[0, :mla_w].astype(BF16)
    w_out2 = w_out[0, mla_w:].astype(BF16)
    w_up_b = w_up[0].astype(BF16)
    w_down_b = w_down[0].astype(BF16)

    inv_freq = 1.0 / (ROPE_THETA ** (jnp.arange(half, dtype=F32) * 2.0 / QK_ROPE))
    invf = jnp.tile(inv_freq, LANE // half).reshape(1, LANE)
    pos128 = jnp.broadcast_to(positions.reshape(m, 1), (m, LANE))

    hn = _rmsnorm(x2, mix_norm_g[0], BF16, "mix_norm")
    proj = _matmul(hn, w_in_b, 1024, 512, "in_proj")
    q, kv, kr = _latent_up(proj, pos128, invf, q_norm_g[0].reshape(1, -1),
                           kv_norm_g[0].reshape(1, -1), wq_ext, wkv)
    mla_out = _attention(q, kv, kr, attn_out_norm_g[0].reshape(1, -1), batch, seq)
    hg_out = _hgrn(proj, lb_logits, hg_norm_g[0].reshape(1, -1), batch, seq, LATENT_PAD)
    h1 = _outproj(mla_out, hg_out, w_out1, w_out2, x2)

    hn2 = _rmsnorm(h1, ffn_norm_g[0], BF16, "ffn_norm")
    act = _upglu(hn2, w_up_b, conv_w[0], conv_b[0].reshape(1, -1), seq)
    h2 = _down(act, w_down_b, h1)
    out = _rmsnorm(h2, final_norm_g, F32, "final_norm")
    return out.reshape(batch, seq, d)
```

```python
import functools
import math

import jax
import jax.numpy as jnp
from jax import lax
from jax.experimental import pallas as pl
from jax.experimental.pallas import tpu as pltpu

F32 = jnp.float32
BF16 = jnp.bfloat16

MLA_HEADS = 16
QK_NOPE = 128
QK_ROPE = 64
V_DIM = 128
Q_LORA = 768
KV_LORA = 512
ROPE_THETA = 10000.0
HG_HEADS = 16
HG_DIM = 128
HG_CHUNK = 64
HG_SUB = 16
CONV_WIDTH = 3
EPS = 1e-6

LANE = 128
SUBLANE = 8
VMEM_LIMIT_BYTES = 60 * 1024 * 1024

LATENT_PAD = 1536
Q_HEAD_W = 2 * LANE
LOG2_E = math.log2(math.e)
QK_LOG2_SCALE = (QK_NOPE + QK_ROPE) ** -0.5 * LOG2_E

_NT = (((1,), (1,)), ((), ()))
_TN = (((0,), (0,)), ((), ()))


def _params(*sem):
    return pltpu.CompilerParams(dimension_semantics=sem, vmem_limit_bytes=VMEM_LIMIT_BYTES)


def _tile(n, pref):
    t = min(n, pref)
    assert n % t == 0, (n, t)
    return t


def _rms_kernel(x_ref, g_ref, o_ref):
    x = x_ref[...]
    ms = jnp.mean(x * x, axis=-1, keepdims=True)
    o_ref[...] = (x * lax.rsqrt(ms + EPS) * g_ref[...]).astype(o_ref.dtype)


def _rmsnorm(x, g, out_dtype, name):
    m, d = x.shape
    tm = _tile(m, 256)
    return pl.pallas_call(
        _rms_kernel,
        out_shape=jax.ShapeDtypeStruct((m, d), out_dtype),
        grid=(m // tm,),
        in_specs=[pl.BlockSpec((tm, d), lambda i: (i, 0)),
                  pl.BlockSpec((1, d), lambda i: (0, 0))],
        out_specs=pl.BlockSpec((tm, d), lambda i: (i, 0)),
        compiler_params=_params("parallel"),
        name=name,
    )(x, g.reshape(1, d))


def _mm_kernel(a_ref, b_ref, o_ref):
    o_ref[...] = jnp.dot(a_ref[...], b_ref[...], preferred_element_type=F32).astype(o_ref.dtype)


def _matmul(a, b, tm, tn, name):
    m, k = a.shape
    _, n = b.shape
    tm, tn = _tile(m, tm), _tile(n, tn)
    return pl.pallas_call(
        _mm_kernel,
        out_shape=jax.ShapeDtypeStruct((m, n), F32),
        grid=(m // tm, n // tn),
        in_specs=[pl.BlockSpec((tm, k), lambda i, j: (i, 0)),
                  pl.BlockSpec((k, tn), lambda i, j: (0, j))],
        out_specs=pl.BlockSpec((tm, tn), lambda i, j: (i, j)),
        compiler_params=_params("parallel", "parallel"),
        name=name,
    )(a, b)


def _latent_kernel(c_ref, pos_ref, invf_ref, qg_ref, kvg_ref, wq_ref, wkv_ref,
                   q_ref, kv_ref, kr_ref):
    c = c_ref[...]
    ang = pos_ref[...].astype(F32) * invf_ref[...]
    lane = lax.broadcasted_iota(jnp.int32, ang.shape, 1)
    sin = jnp.sin(ang)
    cs = jnp.where(lane < QK_ROPE, jnp.cos(ang),
                   jnp.where(lane < QK_ROPE + QK_ROPE // 2, -sin, sin))

    cq = c[:, :Q_LORA]
    cqn = (cq * lax.rsqrt(jnp.mean(cq * cq, axis=-1, keepdims=True) + EPS)
           * qg_ref[...]).astype(BF16)
    ckv = c[:, Q_LORA:Q_LORA + KV_LORA]
    ckvn = (ckv * lax.rsqrt(jnp.mean(ckv * ckv, axis=-1, keepdims=True) + EPS)
            * kvg_ref[...]).astype(BF16)

    for h in range(MLA_HEADS):
        qh = jnp.dot(cqn, wq_ref[:, h * Q_HEAD_W:(h + 1) * Q_HEAD_W],
                     preferred_element_type=F32) * QK_LOG2_SCALE
        q_ref[:, h * Q_HEAD_W:h * Q_HEAD_W + QK_NOPE] = qh[:, :QK_NOPE].astype(BF16)
        q_ref[:, h * Q_HEAD_W + QK_NOPE:(h + 1) * Q_HEAD_W] = (qh[:, QK_NOPE:] * cs).astype(BF16)

    kv_ref[...] = jnp.dot(ckvn, wkv_ref[...], preferred_element_type=F32).astype(BF16)

    t = c[:, Q_LORA + KV_LORA:Q_LORA + KV_LORA + LANE] * cs
    kr_ref[...] = (t + pltpu.roll(t, QK_ROPE, axis=1)).astype(BF16)


def _latent_up(proj, pos128, invf, qg, kvg, wq_ext, wkv):
    m = proj.shape[0]
    ts = _tile(m, 512)
    nq = wq_ext.shape[1]
    nkv = wkv.shape[1]
    return pl.pallas_call(
        _latent_kernel,
        out_shape=(jax.ShapeDtypeStruct((m, nq), BF16),
                   jax.ShapeDtypeStruct((m, nkv), BF16),
                   jax.ShapeDtypeStruct((m, LANE), BF16)),
        grid=(m // ts,),
        in_specs=[pl.BlockSpec((ts, LATENT_PAD), lambda i: (i, 0)),
                  pl.BlockSpec((ts, LANE), lambda i: (i, 0)),
                  pl.BlockSpec((1, LANE), lambda i: (0, 0)),
                  pl.BlockSpec((1, Q_LORA), lambda i: (0, 0)),
                  pl.BlockSpec((1, KV_LORA), lambda i: (0, 0)),
                  pl.BlockSpec((Q_LORA, nq), lambda i: (0, 0)),
                  pl.BlockSpec((KV_LORA, nkv), lambda i: (0, 0))],
        out_specs=(pl.BlockSpec((ts, nq), lambda i: (i, 0)),
                   pl.BlockSpec((ts, nkv), lambda i: (i, 0)),
                   pl.BlockSpec((ts, LANE), lambda i: (i, 0))),
        compiler_params=_params("parallel"),
        name="latent_up",
    )(proj, pos128, invf, qg, kvg, wq_ext, wkv)


def _attn_kernel(q_ref, kn_ref, v_ref, kr_ref, g_ref, o_ref, kfull_ref, *, tq, tk):
    qi = pl.program_id(2)

    @pl.when(qi == 0)
    def _():
        kfull_ref[:, :QK_NOPE] = kn_ref[...]
        kfull_ref[:, QK_NOPE:] = kr_ref[...]

    nsub = tq // tk
    qs = [q_ref[j * tk:(j + 1) * tk, :] for j in range(nsub)]

    def update(q, k, v, carry, masked):
        m, l, acc = carry
        s = lax.dot_general(q, k, _NT, preferred_element_type=F32)
        if masked:
            row = lax.broadcasted_iota(jnp.int32, s.shape, 0)
            col = lax.broadcasted_iota(jnp.int32, s.shape, 1)
            s = jnp.where(col <= row, s, -jnp.inf)
        m_new = jnp.maximum(m, jnp.max(s, axis=-1, keepdims=True))
        p = jnp.exp2(s - m_new)
        alpha = jnp.exp2(m - m_new)
        l = alpha * l + jnp.sum(p, axis=-1, keepdims=True)
        acc = alpha * acc + jnp.dot(p.astype(BF16), v, preferred_element_type=F32)
        return m_new, l, acc

    def load_kv(kb):
        start = pl.multiple_of(kb * tk, tk)
        return kfull_ref[pl.ds(start, tk), :], v_ref[pl.ds(start, tk), :]

    def body(kb, carries):
        k, v = load_kv(kb)
        return tuple(update(qs[j], k, v, carries[j], False) for j in range(nsub))

    init = tuple((jnp.full((tk, 1), -jnp.inf, F32), jnp.zeros((tk, 1), F32),
                  jnp.zeros((tk, V_DIM), F32)) for _ in range(nsub))
    carries = list(lax.fori_loop(0, qi * nsub, body, init))
    for jb in range(nsub):
        k, v = load_kv(qi * nsub + jb)
        for j in range(jb, nsub):
            carries[j] = update(qs[j], k, v, carries[j], j == jb)

    for j in range(nsub):
        _, l, acc = carries[j]
        o = acc / l
        ms = jnp.mean(o * o, axis=-1, keepdims=True)
        o_ref[j * tk:(j + 1) * tk, :] = (o * lax.rsqrt(ms + EPS) * g_ref[...]).astype(o_ref.dtype)


def _attention(q, kv, kr, g, batch, seq):
    m = q.shape[0]
    tk = _tile(seq, 512)
    tq = _tile(seq, 2 * tk)
    nq = seq // tq
    return pl.pallas_call(
        functools.partial(_attn_kernel, tq=tq, tk=tk),
        out_shape=jax.ShapeDtypeStruct((m, MLA_HEADS * V_DIM), BF16),
        grid=(batch, MLA_HEADS, nq),
        in_specs=[pl.BlockSpec((tq, Q_HEAD_W), lambda b, h, i: (b * nq + i, h)),
                  pl.BlockSpec((seq, QK_NOPE), lambda b, h, i: (b, 2 * h)),
                  pl.BlockSpec((seq, V_DIM), lambda b, h, i: (b, 2 * h + 1)),
                  pl.BlockSpec((seq, LANE), lambda b, h, i: (b, 0)),
                  pl.BlockSpec((1, V_DIM), lambda b, h, i: (0, h))],
        out_specs=pl.BlockSpec((tq, V_DIM), lambda b, h, i: (b * nq + i, h)),
        scratch_shapes=[pltpu.VMEM((seq, Q_HEAD_W), BF16)],
        compiler_params=_params("parallel", "parallel", "arbitrary"),
        name="mla_attention",
    )(q, kv, kv, kr, g)


def _hgrn_chunk(z, hq, hi, gate, st, a_ref, log_lb, log1m_lb, gn, tri_ones, causal):
    sp = jnp.log(1.0 + jnp.exp(-jnp.abs(z)))
    log_sig = jnp.minimum(z, 0.0) - sp
    log_sig_neg = jnp.minimum(-z, 0.0) - sp
    b = log1m_lb + log_sig
    log_f = jnp.maximum(log_lb, b) + jnp.log(1.0 + jnp.exp(-jnp.abs(log_lb - b)))
    kk = jnp.exp(log1m_lb + log_sig_neg)

    g = log_f * LOG2_E
    g1 = g.astype(BF16)
    r1 = g - g1.astype(F32)
    g2 = r1.astype(BF16)
    g3 = (r1 - g2.astype(F32)).astype(BF16)
    gs = jnp.dot(tri_ones, jnp.concatenate([g1, g2, g3], axis=1), preferred_element_type=F32)
    G = gs[:, :LANE] + gs[:, LANE:2 * LANE] + gs[:, 2 * LANE:]

    qs = hq * (HG_DIM ** -0.5)
    vb = hi.astype(BF16)
    inter = lax.dot_general((qs * jnp.exp2(G)).astype(BF16), st.astype(BF16), _NT,
                            preferred_element_type=F32)

    for i in range(HG_CHUNK // HG_SUB):
        lo = i * HG_SUB
        Gi = G[lo:lo + HG_SUB]
        qi = qs[lo:lo + HG_SUB]
        ki = kk[lo:lo + HG_SUB]
        if i > 0:
            bnd = G[lo - 1:lo]
            qh = (qi * jnp.exp2(Gi - bnd)).astype(BF16)
            kt = (kk[:lo] * jnp.exp2(bnd - G[:lo])).astype(BF16)
            a_ref[lo:lo + HG_SUB, 0:lo] = lax.dot_general(qh, kt, _NT, preferred_element_type=F32)
        for s in range(HG_SUB):
            e = jnp.exp2(Gi - Gi[s:s + 1])
            a_ref[lo:lo + HG_SUB, lo + s:lo + s + 1] = jnp.sum(qi * ki[s:s + 1] * e, axis=-1,
                                                               keepdims=True)
    a = jnp.where(causal, a_ref[...], 0.0).astype(BF16)
    o = inter + jnp.dot(a, vb, preferred_element_type=F32)

    g_last = G[HG_CHUNK - 1:HG_CHUNK]
    kdec = (kk * jnp.exp2(g_last - G)).astype(BF16)
    st_new = st * jnp.exp2(g_last) + lax.dot_general(vb, kdec, _TN, preferred_element_type=F32)

    ms = jnp.mean(o * o, axis=-1, keepdims=True)
    y = o * lax.rsqrt(ms + EPS) * gn
    return y * (gate * jax.nn.sigmoid(gate)), st_new


def _hgrn_kernel(hq_ref, hf_ref, hi_ref, hg_ref, lb_ref, g_ref, o_ref, st_ref, a_ref, *, ts, hpb):
    si = pl.program_id(2)

    @pl.when(si == 0)
    def _():
        st_ref[...] = jnp.zeros_like(st_ref)

    a_ref[...] = jnp.zeros_like(a_ref)

    l0 = lb_ref[0:1, :]
    l1 = lb_ref[1:2, :]
    lmax = jnp.maximum(l0, l1)
    e0 = jnp.exp(l0 - lmax)
    e1 = jnp.exp(l1 - lmax)
    lb = e0 / (e0 + e1)
    log_lb = jnp.log(lb)
    log1m_lb = jnp.log1p(-lb)
    gn = g_ref[...]

    causal = (lax.broadcasted_iota(jnp.int32, (HG_CHUNK, HG_CHUNK), 0)
              >= lax.broadcasted_iota(jnp.int32, (HG_CHUNK, HG_CHUNK), 1))
    tri_ones = jnp.where(causal, 1.0, 0.0).astype(BF16)

    def chunk_body(ci, sts):
        rows = pl.ds(pl.multiple_of(ci * HG_CHUNK, HG_CHUNK), HG_CHUNK)
        new = []
        for j in range(hpb):
            cols = slice(j * HG_DIM, (j + 1) * HG_DIM)
            y, st = _hgrn_chunk(hf_ref[rows, cols], hq_ref[rows, cols], hi_ref[rows, cols],
                                hg_ref[rows, cols], sts[j], a_ref.at[j],
                                log_lb[:, cols], log1m_lb[:, cols], gn[:, cols], tri_ones, causal)
            o_ref[rows, cols] = y.astype(o_ref.dtype)
            new.append(st)
        return tuple(new)

    sts = lax.fori_loop(0, ts // HG_CHUNK, chunk_body, tuple(st_ref[j] for j in range(hpb)))
    for j in range(hpb):
        st_ref[j] = sts[j]


def _hgrn(proj, lb_logits, g, batch, seq, col0):
    m = proj.shape[0]
    ts = _tile(seq, 512)
    ns = seq // ts
    hpb = 4
    bw = hpb * HG_DIM
    assert col0 % bw == 0 and HG_HEADS % hpb == 0
    nb = HG_HEADS // hpb
    c0 = col0 // bw

    def spec(j):
        return pl.BlockSpec((ts, bw), lambda b, h, i: (b * ns + i, c0 + j * nb + h))

    return pl.pallas_call(
        functools.partial(_hgrn_kernel, ts=ts, hpb=hpb),
        out_shape=jax.ShapeDtypeStruct((m, HG_HEADS * HG_DIM), BF16),
        grid=(batch, nb, ns),
        in_specs=[spec(0), spec(1), spec(2), spec(3),
                  pl.BlockSpec((2, bw), lambda b, h, i: (0, h)),
                  pl.BlockSpec((1, bw), lambda b, h, i: (0, h))],
        out_specs=pl.BlockSpec((ts, bw), lambda b, h, i: (b * ns + i, h)),
        scratch_shapes=[pltpu.VMEM((hpb, HG_DIM, HG_DIM), F32),
                        pltpu.VMEM((hpb, HG_CHUNK, HG_CHUNK), F32)],
        compiler_params=_params("parallel", "parallel", "arbitrary"),
        name="hgrn2",
    )(proj, proj, proj, proj, lb_logits, g)


def _outproj_kernel(a1_ref, a2_ref, w1_ref, w2_ref, r_ref, o_ref):
    acc = jnp.dot(a1_ref[...], w1_ref[...], preferred_element_type=F32)
    acc = acc + jnp.dot(a2_ref[...], w2_ref[...], preferred_element_type=F32)
    o_ref[...] = r_ref[...] + acc


def _outproj(a1, a2, w1, w2, res):
    m, k1 = a1.shape
    k2 = a2.shape[1]
    n = w1.shape[1]
    tm, tn = _tile(m, 1024), _tile(n, 1024)
    return pl.pallas_call(
        _outproj_kernel,
        out_shape=jax.ShapeDtypeStruct((m, n), F32),
        grid=(m // tm, n // tn),
        in_specs=[pl.BlockSpec((tm, k1), lambda i, j: (i, 0)),
                  pl.BlockSpec((tm, k2), lambda i, j: (i, 0)),
                  pl.BlockSpec((k1, tn), lambda i, j: (0, j)),
                  pl.BlockSpec((k2, tn), lambda i, j: (0, j)),
                  pl.BlockSpec((tm, tn), lambda i, j: (i, j))],
        out_specs=pl.BlockSpec((tm, tn), lambda i, j: (i, j)),
        compiler_params=_params("parallel", "parallel"),
        name="out_proj",
    )(a1, a2, w1, w2, res)


def _upglu_kernel(x_ref, wg_ref, wu_ref, cw_ref, cb_ref, o_ref, halo_ref, *, tm, tiles_per_seq):
    mi = pl.program_id(0)
    n = pl.program_id(1)
    x = x_ref[...]
    gate = jnp.dot(x, wg_ref[...], preferred_element_type=F32)
    up = jnp.dot(x, wu_ref[...], preferred_element_type=F32)

    prev = halo_ref[n]
    prev = jnp.where(mi % tiles_per_seq == 0, 0.0, prev)
    p1 = prev[SUBLANE - 1:SUBLANE]
    p2 = prev[SUBLANE - 2:SUBLANE - 1]
    row = lax.broadcasted_iota(jnp.int32, gate.shape, 0)
    g1 = jnp.where(row == 0, p1, pltpu.roll(gate, 1, axis=0))
    g2 = jnp.where(row == 0, p2, jnp.where(row == 1, p1, pltpu.roll(gate, 2, axis=0)))
    cw = cw_ref[...]
    conv = cb_ref[...] + cw[0:1] * g2
    conv = conv + cw[1:2] * g1
    conv = conv + cw[2:3] * gate
    o_ref[...] = (conv * jax.nn.sigmoid(conv) * up).astype(o_ref.dtype)
    halo_ref[n] = gate[tm - SUBLANE:tm]


def _upglu(x, w_up, conv_w, conv_b, seq):
    m, k = x.shape
    dff = w_up.shape[1] // 2
    tm = _tile(seq, 2048)
    tn = 2 * LANE
    assert dff % tn == 0
    nn = dff // tn
    return pl.pallas_call(
        functools.partial(_upglu_kernel, tm=tm, tiles_per_seq=seq // tm),
        out_shape=jax.ShapeDtypeStruct((m, dff), BF16),
        grid=(m // tm, nn),
        in_specs=[pl.BlockSpec((tm, k), lambda i, j: (i, 0)),
                  pl.BlockSpec((k, tn), lambda i, j: (0, j)),
                  pl.BlockSpec((k, tn), lambda i, j: (0, nn + j)),
                  pl.BlockSpec((CONV_WIDTH, tn), lambda i, j: (0, j)),
                  pl.BlockSpec((1, tn), lambda i, j: (0, j))],
        out_specs=pl.BlockSpec((tm, tn), lambda i, j: (i, j)),
        scratch_shapes=[pltpu.VMEM((nn, SUBLANE, tn), F32)],
        compiler_params=_params("arbitrary", "arbitrary"),
        name="up_convglu",
    )(x, w_up, w_up, conv_w, conv_b)


def _down_kernel(a_ref, w_ref, r_ref, o_ref):
    o_ref[...] = r_ref[...] + jnp.dot(a_ref[...], w_ref[...], preferred_element_type=F32)


def _down(a, w, res):
    m, k = a.shape
    n = w.shape[1]
    tm, tn = _tile(m, 512), _tile(n, 512)
    return pl.pallas_call(
        _down_kernel,
        out_shape=jax.ShapeDtypeStruct((m, n), F32),
        grid=(m // tm, n // tn),
        in_specs=[pl.BlockSpec((tm, k), lambda i, j: (i, 0)),
                  pl.BlockSpec((k, tn), lambda i, j: (0, j)),
                  pl.BlockSpec((tm, tn), lambda i, j: (i, j))],
        out_specs=pl.BlockSpec((tm, tn), lambda i, j: (i, j)),
        compiler_params=_params("parallel", "parallel"),
        name="down_proj",
    )(a, w, res)


def kernel(x, positions, mix_norm_g, w_in, q_norm_g, w_uq, kv_norm_g, w_ukv, attn_out_norm_g,
           lb_logits, hg_norm_g, w_out, ffn_norm_g, w_up, conv_w, conv_b, w_down, final_norm_g):
    batch, seq, d = x.shape
    assert w_in.shape[0] == 1 and lb_logits.shape[0] == 2, "single-layer block expected"
    m = batch * seq
    x2 = x.reshape(m, d)
    half = QK_ROPE // 2

    w = w_in[0]
    o1, o2, o3 = Q_LORA, Q_LORA + KV_LORA, Q_LORA + KV_LORA + QK_ROPE
    wkr = w[:, o2:o3]
    wkr_sw = jnp.concatenate([wkr[:, half:], wkr[:, :half]], axis=1)
    w_in_b = jnp.concatenate(
        [w[:, :o2], wkr, wkr_sw, jnp.zeros((d, LATENT_PAD - o3 - QK_ROPE), w.dtype), w[:, o3:]],
        axis=1).astype(BF16)

    wq = w_uq[0].reshape(Q_LORA, MLA_HEADS, QK_NOPE + QK_ROPE)
    wq_rope = wq[..., QK_NOPE:]
    wq_ext = jnp.concatenate(
        [wq, wq_rope[..., half:], wq_rope[..., :half]], axis=-1
    ).reshape(Q_LORA, MLA_HEADS * Q_HEAD_W).astype(BF16)
    wkv = w_ukv[0].astype(BF16)
    mla_w = MLA_HEADS * V_DIM
    w_out1 = w_out[0, :mla_w].astype(BF16)
    w_out2 = w_out[0, mla_w:].astype(BF16)
    w_up_b = w_up[0].astype(BF16)
    w_down_b = w_down[0].astype(BF16)

    inv_freq = 1.0 / (ROPE_THETA ** (jnp.arange(half, dtype=F32) * 2.0 / QK_ROPE))
    invf = jnp.tile(inv_freq, LANE // half).reshape(1, LANE)
    pos128 = jnp.broadcast_to(positions.reshape(m, 1), (m, LANE))

    hn = _rmsnorm(x2, mix_norm_g[0], BF16, "mix_norm")
    proj = _matmul(hn, w_in_b, 2048, 512, "in_proj")
    q, kv, kr = _latent_up(proj, pos128, invf, q_norm_g[0].reshape(1, -1),
                           kv_norm_g[0].reshape(1, -1), wq_ext, wkv)
    mla_out = _attention(q, kv, kr, attn_out_norm_g[0].reshape(1, -1), batch, seq)
    hg_out = _hgrn(proj, lb_logits, hg_norm_g[0].reshape(1, -1), batch, seq, LATENT_PAD)
    h1 = _outproj(mla_out, hg_out, w_out1, w_out2, x2)

    hn2 = _rmsnorm(h1, ffn_norm_g[0], BF16, "ffn_norm")
    act = _upglu(hn2, w_up_b, conv_w[0], conv_b[0].reshape(1, -1), seq)
    h2 = _down(act, w_down_b, h1)
    out = _rmsnorm(h2, final_norm_g, F32, "final_norm")
    return out.reshape(batch, seq, d)
```

```python
import functools
import math

import jax
import jax.numpy as jnp
from jax import lax
from jax.experimental import pallas as pl
from jax.experimental.pallas import tpu as pltpu

F32 = jnp.float32
BF16 = jnp.bfloat16

MLA_HEADS = 16
QK_NOPE = 128
QK_ROPE = 64
V_DIM = 128
Q_LORA = 768
KV_LORA = 512
ROPE_THETA = 10000.0
HG_HEADS = 16
HG_DIM = 128
HG_CHUNK = 64
HG_SUB = 16
CONV_WIDTH = 3
EPS = 1e-6

LANE = 128
SUBLANE = 8
VMEM_LIMIT_BYTES = 60 * 1024 * 1024

LATENT_PAD = 1536
Q_HEAD_W = 2 * LANE
LOG2_E = math.log2(math.e)
QK_LOG2_SCALE = (QK_NOPE + QK_ROPE) ** -0.5 * LOG2_E

_NT = (((1,), (1,)), ((), ()))
_TN = (((0,), (0,)), ((), ()))


def _params(*sem):
    return pltpu.CompilerParams(dimension_semantics=sem, vmem_limit_bytes=VMEM_LIMIT_BYTES)


def _tile(n, pref):
    t = min(n, pref)
    assert n % t == 0, (n, t)
    return t


def _rms_kernel(x_ref, g_ref, o_ref):
    x = x_ref[...]
    ms = jnp.mean(x * x, axis=-1, keepdims=True)
    o_ref[...] = (x * lax.rsqrt(ms + EPS) * g_ref[...]).astype(o_ref.dtype)


def _rmsnorm(x, g, out_dtype, name):
    m, d = x.shape
    tm = _tile(m, 256)
    return pl.pallas_call(
        _rms_kernel,
        out_shape=jax.ShapeDtypeStruct((m, d), out_dtype),
        grid=(m // tm,),
        in_specs=[pl.BlockSpec((tm, d), lambda i: (i, 0)),
                  pl.BlockSpec((1, d), lambda i: (0, 0))],
        out_specs=pl.BlockSpec((tm, d), lambda i: (i, 0)),
        compiler_params=_params("parallel"),
        name=name,
    )(x, g.reshape(1, d))


def _mm_kernel(a_ref, b_ref, o_ref):
    o_ref[...] = jnp.dot(a_ref[...], b_ref[...], preferred_element_type=F32).astype(o_ref.dtype)


def _matmul(a, b, tm, tn, name):
    m, k = a.shape
    _, n = b.shape
    tm, tn = _tile(m, tm), _tile(n, tn)
    return pl.pallas_call(
        _mm_kernel,
        out_shape=jax.ShapeDtypeStruct((m, n), F32),
        grid=(m // tm, n // tn),
        in_specs=[pl.BlockSpec((tm, k), lambda i, j: (i, 0)),
                  pl.BlockSpec((k, tn), lambda i, j: (0, j))],
        out_specs=pl.BlockSpec((tm, tn), lambda i, j: (i, j)),
        compiler_params=_params("parallel", "parallel"),
        name=name,
    )(a, b)


def _latent_kernel(c_ref, pos_ref, invf_ref, qg_ref, kvg_ref, wq_ref, wkv_ref,
                   q_ref, kv_ref, kr_ref):
    c = c_ref[...]
    ang = pos_ref[...].astype(F32) * invf_ref[...]
    lane = lax.broadcasted_iota(jnp.int32, ang.shape, 1)
    sin = jnp.sin(ang)
    cs = jnp.where(lane < QK_ROPE, jnp.cos(ang),
                   jnp.where(lane < QK_ROPE + QK_ROPE // 2, -sin, sin))

    cq = c[:, :Q_LORA]
    cqn = (cq * lax.rsqrt(jnp.mean(cq * cq, axis=-1, keepdims=True) + EPS)
           * qg_ref[...]).astype(BF16)
    ckv = c[:, Q_LORA:Q_LORA + KV_LORA]
    ckvn = (ckv * lax.rsqrt(jnp.mean(ckv * ckv, axis=-1, keepdims=True) + EPS)
            * kvg_ref[...]).astype(BF16)

    for h in range(MLA_HEADS):
        qh = jnp.dot(cqn, wq_ref[:, h * Q_HEAD_W:(h + 1) * Q_HEAD_W],
                     preferred_element_type=F32) * QK_LOG2_SCALE
        q_ref[:, h * Q_HEAD_W:h * Q_HEAD_W + QK_NOPE] = qh[:, :QK_NOPE].astype(BF16)
        q_ref[:, h * Q_HEAD_W + QK_NOPE:(h + 1) * Q_HEAD_W] = (qh[:, QK_NOPE:] * cs).astype(BF16)

    kv_ref[...] = jnp.dot(ckvn, wkv_ref[...], preferred_element_type=F32).astype(BF16)

    t = c[:, Q_LORA + KV_LORA:Q_LORA + KV_LORA + LANE] * cs
    kr_ref[...] = (t + pltpu.roll(t, QK_ROPE, axis=1)).astype(BF16)


def _latent_up(proj, pos128, invf, qg, kvg, wq_ext, wkv):
    m = proj.shape[0]
    ts = _tile(m, 512)
    nq = wq_ext.shape[1]
    nkv = wkv.shape[1]
    return pl.pallas_call(
        _latent_kernel,
        out_shape=(jax.ShapeDtypeStruct((m, nq), BF16),
                   jax.ShapeDtypeStruct((m, nkv), BF16),
                   jax.ShapeDtypeStruct((m, LANE), BF16)),
        grid=(m // ts,),
        in_specs=[pl.BlockSpec((ts, LATENT_PAD), lambda i: (i, 0)),
                  pl.BlockSpec((ts, LANE), lambda i: (i, 0)),
                  pl.BlockSpec((1, LANE), lambda i: (0, 0)),
                  pl.BlockSpec((1, Q_LORA), lambda i: (0, 0)),
                  pl.BlockSpec((1, KV_LORA), lambda i: (0, 0)),
                  pl.BlockSpec((Q_LORA, nq), lambda i: (0, 0)),
                  pl.BlockSpec((KV_LORA, nkv), lambda i: (0, 0))],
        out_specs=(pl.BlockSpec((ts, nq), lambda i: (i, 0)),
                   pl.BlockSpec((ts, nkv), lambda i: (i, 0)),
                   pl.BlockSpec((ts, LANE), lambda i: (i, 0))),
        compiler_params=_params("parallel"),
        name="latent_up",
    )(proj, pos128, invf, qg, kvg, wq_ext, wkv)


def _attn_kernel(q_ref, kn_ref, v_ref, kr_ref, g_ref, o_ref, kfull_ref, *, tq, tk):
    qi = pl.program_id(2)

    @pl.when(qi == 0)
    def _():
        kfull_ref[:, :QK_NOPE] = kn_ref[...]
        kfull_ref[:, QK_NOPE:] = kr_ref[...]

    nsub = tq // tk
    qs = [q_ref[j * tk:(j + 1) * tk, :] for j in range(nsub)]

    def update(q, k, v, carry, masked):
        m, l, acc = carry
        s = lax.dot_general(q, k, _NT, preferred_element_type=F32)
        if masked:
            row = lax.broadcasted_iota(jnp.int32, s.shape, 0)
            col = lax.broadcasted_iota(jnp.int32, s.shape, 1)
            s = jnp.where(col <= row, s, -jnp.inf)
        m_new = jnp.maximum(m, jnp.max(s, axis=-1, keepdims=True))
        p = jnp.exp2(s - m_new)
        alpha = jnp.exp2(m - m_new)
        l = alpha * l + jnp.sum(p, axis=-1, keepdims=True)
        acc = alpha * acc + jnp.dot(p.astype(BF16), v, preferred_element_type=F32)
        return m_new, l, acc

    def load_kv(kb):
        start = pl.multiple_of(kb * tk, tk)
        return kfull_ref[pl.ds(start, tk), :], v_ref[pl.ds(start, tk), :]

    def body(kb, carries):
        k, v = load_kv(kb)
        return tuple(update(qs[j], k, v, carries[j], False) for j in range(nsub))

    init = tuple((jnp.full((tk, 1), -jnp.inf, F32), jnp.zeros((tk, 1), F32),
                  jnp.zeros((tk, V_DIM), F32)) for _ in range(nsub))
    carries = list(lax.fori_loop(0, qi * nsub, body, init))
    for jb in range(nsub):
        k, v = load_kv(qi * nsub + jb)
        for j in range(jb, nsub):
            carries[j] = update(qs[j], k, v, carries[j], j == jb)

    for j in range(nsub):
        _, l, acc = carries[j]
        o = acc / l
        ms = jnp.mean(o * o, axis=-1, keepdims=True)
        o_ref[j * tk:(j + 1) * tk, :] = (o * lax.rsqrt(ms + EPS) * g_ref[...]).astype(o_ref.dtype)


def _attention(q, kv, kr, g, batch, seq):
    m = q.shape[0]
    tk = _tile(seq, 512)
    tq = _tile(seq, 2 * tk)
    nq = seq // tq
    return pl.pallas_call(
        functools.partial(_attn_kernel, tq=tq, tk=tk),
        out_shape=jax.ShapeDtypeStruct((m, MLA_HEADS * V_DIM), BF16),
        grid=(batch, MLA_HEADS, nq),
        in_specs=[pl.BlockSpec((tq, Q_HEAD_W), lambda b, h, i: (b * nq + i, h)),
                  pl.BlockSpec((seq, QK_NOPE), lambda b, h, i: (b, 2 * h)),
                  pl.BlockSpec((seq, V_DIM), lambda b, h, i: (b, 2 * h + 1)),
                  pl.BlockSpec((seq, LANE), lambda b, h, i: (b, 0)),
                  pl.BlockSpec((1, V_DIM), lambda b, h, i: (0, h))],
        out_specs=pl.BlockSpec((tq, V_DIM), lambda b, h, i: (b * nq + i, h)),
        scratch_shapes=[pltpu.VMEM((seq, Q_HEAD_W), BF16)],
        compiler_params=_params("parallel", "parallel", "arbitrary"),
        name="mla_attention",
    )(q, kv, kv, kr, g)


def _hgrn_chunk(z, hq, hi, gate, st, a_ref, log_lb, log1m_lb, gn, tri_ones, causal):
    sp = jnp.log(1.0 + jnp.exp(-jnp.abs(z)))
    log_sig = jnp.minimum(z, 0.0) - sp
    log_sig_neg = jnp.minimum(-z, 0.0) - sp
    b = log1m_lb + log_sig
    log_f = jnp.maximum(log_lb, b) + jnp.log(1.0 + jnp.exp(-jnp.abs(log_lb - b)))
    kk = jnp.exp(log1m_lb + log_sig_neg)

    g = log_f * LOG2_E
    g1 = g.astype(BF16)
    r1 = g - g1.astype(F32)
    g2 = r1.astype(BF16)
    g3 = (r1 - g2.astype(F32)).astype(BF16)
    gs = jnp.dot(tri_ones, jnp.concatenate([g1, g2, g3], axis=1), preferred_element_type=F32)
    G = gs[:, :LANE] + gs[:, LANE:2 * LANE] + gs[:, 2 * LANE:]

    qs = hq * (HG_DIM ** -0.5)
    vb = hi.astype(BF16)
    inter = lax.dot_general((qs * jnp.exp2(G)).astype(BF16), st.astype(BF16), _NT,
                            preferred_element_type=F32)

    for i in range(HG_CHUNK // HG_SUB):
        lo = i * HG_SUB
        Gi = G[lo:lo + HG_SUB]
        qi = qs[lo:lo + HG_SUB]
        ki = kk[lo:lo + HG_SUB]
        if i > 0:
            bnd = G[lo - 1:lo]
            qh = (qi * jnp.exp2(Gi - bnd)).astype(BF16)
            kt = (kk[:lo] * jnp.exp2(bnd - G[:lo])).astype(BF16)
            a_ref[lo:lo + HG_SUB, 0:lo] = lax.dot_general(qh, kt, _NT, preferred_element_type=F32)
        for s in range(HG_SUB):
            e = jnp.exp2(Gi - Gi[s:s + 1])
            a_ref[lo:lo + HG_SUB, lo + s:lo + s + 1] = jnp.sum(qi * ki[s:s + 1] * e, axis=-1,
                                                               keepdims=True)
    a = jnp.where(causal, a_ref[...], 0.0).astype(BF16)
    o = inter + jnp.dot(a, vb, preferred_element_type=F32)

    g_last = G[HG_CHUNK - 1:HG_CHUNK]
    kdec = (kk * jnp.exp2(g_last - G)).astype(BF16)
    st_new = st * jnp.exp2(g_last) + lax.dot_general(vb, kdec, _TN, preferred_element_type=F32)

    ms = jnp.mean(o * o, axis=-1, keepdims=True)
    y = o * lax.rsqrt(ms + EPS) * gn
    return y * (gate * jax.nn.sigmoid(gate)), st_new


def _hgrn_kernel(hq_ref, hf_ref, hi_ref, hg_ref, lb_ref, g_ref, o_ref, st_ref, a_ref, *, ts, hpb):
    si = pl.program_id(2)

    @pl.when(si == 0)
    def _():
        st_ref[...] = jnp.zeros_like(st_ref)

    a_ref[...] = jnp.zeros_like(a_ref)

    l0 = lb_ref[0:1, :]
    l1 = lb_ref[1:2, :]
    lmax = jnp.maximum(l0, l1)
    e0 = jnp.exp(l0 - lmax)
    e1 = jnp.exp(l1 - lmax)
    lb = e0 / (e0 + e1)
    log_lb = jnp.log(lb)
    log1m_lb = jnp.log1p(-lb)
    gn = g_ref[...]

    causal = (lax.broadcasted_iota(jnp.int32, (HG_CHUNK, HG_CHUNK), 0)
              >= lax.broadcasted_iota(jnp.int32, (HG_CHUNK, HG_CHUNK), 1))
    tri_ones = jnp.where(causal, 1.0, 0.0).astype(BF16)

    def chunk_body(ci, sts):
        rows = pl.ds(pl.multiple_of(ci * HG_CHUNK, HG_CHUNK), HG_CHUNK)
        new = []
        for j in range(hpb):
            cols = slice(j * HG_DIM, (j + 1) * HG_DIM)
            y, st = _hgrn_chunk(hf_ref[rows, cols], hq_ref[rows, cols], hi_ref[rows, cols],
                                hg_ref[rows, cols], sts[j], a_ref.at[j],
                                log_lb[:, cols], log1m_lb[:, cols], gn[:, cols], tri_ones, causal)
            o_ref[rows, cols] = y.astype(o_ref.dtype)
            new.append(st)
        return tuple(new)

    sts = lax.fori_loop(0, ts // HG_CHUNK, chunk_body, tuple(st_ref[j] for j in range(hpb)))
    for j in range(hpb):
        st_ref[j] = sts[j]


def _hgrn(proj, lb_logits, g, batch, seq):
    m = proj.shape[0]
    ts = _tile(seq, 512)
    ns = seq // ts
    hpb = 8
    bw = hpb * HG_DIM
    assert HG_HEADS % hpb == 0
    nb = HG_HEADS // hpb

    def spec(j):
        return pl.BlockSpec((ts, bw), lambda b, h, i: (b * ns + i, j * nb + h))

    return pl.pallas_call(
        functools.partial(_hgrn_kernel, ts=ts, hpb=hpb),
        out_shape=jax.ShapeDtypeStruct((m, HG_HEADS * HG_DIM), BF16),
        grid=(batch, nb, ns),
        in_specs=[spec(0), spec(1), spec(2), spec(3),
                  pl.BlockSpec((2, bw), lambda b, h, i: (0, h)),
                  pl.BlockSpec((1, bw), lambda b, h, i: (0, h))],
        out_specs=pl.BlockSpec((ts, bw), lambda b, h, i: (b * ns + i, h)),
        scratch_shapes=[pltpu.VMEM((hpb, HG_DIM, HG_DIM), F32),
                        pltpu.VMEM((hpb, HG_CHUNK, HG_CHUNK), F32)],
        compiler_params=_params("parallel", "parallel", "arbitrary"),
        name="hgrn2",
    )(proj, proj, proj, proj, lb_logits, g)


def _outproj_kernel(a1_ref, a2_ref, w1_ref, w2_ref, r_ref, o_ref):
    acc = jnp.dot(a1_ref[...], w1_ref[...], preferred_element_type=F32)
    acc = acc + jnp.dot(a2_ref[...], w2_ref[...], preferred_element_type=F32)
    o_ref[...] = r_ref[...] + acc


def _outproj(a1, a2, w1, w2, res):
    m, k1 = a1.shape
    k2 = a2.shape[1]
    n = w1.shape[1]
    tm, tn = _tile(m, 1024), _tile(n, 1024)
    return pl.pallas_call(
        _outproj_kernel,
        out_shape=jax.ShapeDtypeStruct((m, n), F32),
        grid=(m // tm, n // tn),
        in_specs=[pl.BlockSpec((tm, k1), lambda i, j: (i, 0)),
                  pl.BlockSpec((tm, k2), lambda i, j: (i, 0)),
                  pl.BlockSpec((k1, tn), lambda i, j: (0, j)),
                  pl.BlockSpec((k2, tn), lambda i, j: (0, j)),
                  pl.BlockSpec((tm, tn), lambda i, j: (i, j))],
        out_specs=pl.BlockSpec((tm, tn), lambda i, j: (i, j)),
        compiler_params=_params("parallel", "parallel"),
        name="out_proj",
    )(a1, a2, w1, w2, res)


def _upglu_kernel(x_ref, wg_ref, wu_ref, cw_ref, cb_ref, o_ref, halo_ref, *, tm, tiles_per_seq):
    mi = pl.program_id(0)
    n = pl.program_id(1)
    x = x_ref[...]
    gate = jnp.dot(x, wg_ref[...], preferred_element_type=F32)
    up = jnp.dot(x, wu_ref[...], preferred_element_type=F32)

    prev = halo_ref[n]
    prev = jnp.where(mi % tiles_per_seq == 0, 0.0, prev)
    p1 = prev[SUBLANE - 1:SUBLANE]
    p2 = prev[SUBLANE - 2:SUBLANE - 1]
    row = lax.broadcasted_iota(jnp.int32, gate.shape, 0)
    g1 = jnp.where(row == 0, p1, pltpu.roll(gate, 1, axis=0))
    g2 = jnp.where(row == 0, p2, jnp.where(row == 1, p1, pltpu.roll(gate, 2, axis=0)))
    cw = cw_ref[...]
    conv = cb_ref[...] + cw[0:1] * g2
    conv = conv + cw[1:2] * g1
    conv = conv + cw[2:3] * gate
    o_ref[...] = (conv * jax.nn.sigmoid(conv) * up).astype(o_ref.dtype)
    halo_ref[n] = gate[tm - SUBLANE:tm]


def _upglu(x, w_up, conv_w, conv_b, seq):
    m, k = x.shape
    dff = w_up.shape[1] // 2
    tm = _tile(seq, 2048)
    tn = 2 * LANE
    assert dff % tn == 0
    nn = dff // tn
    return pl.pallas_call(
        functools.partial(_upglu_kernel, tm=tm, tiles_per_seq=seq // tm),
        out_shape=jax.ShapeDtypeStruct((m, dff), BF16),
        grid=(m // tm, nn),
        in_specs=[pl.BlockSpec((tm, k), lambda i, j: (i, 0)),
                  pl.BlockSpec((k, tn), lambda i, j: (0, j)),
                  pl.BlockSpec((k, tn), lambda i, j: (0, nn + j)),
                  pl.BlockSpec((CONV_WIDTH, tn), lambda i, j: (0, j)),
                  pl.BlockSpec((1, tn), lambda i, j: (0, j))],
        out_specs=pl.BlockSpec((tm, tn), lambda i, j: (i, j)),
        scratch_shapes=[pltpu.VMEM((nn, SUBLANE, tn), F32)],
        compiler_params=_params("arbitrary", "arbitrary"),
        name="up_convglu",
    )(x, w_up, w_up, conv_w, conv_b)


def _down_kernel(a_ref, w_ref, r_ref, o_ref):
    o_ref[...] = r_ref[...] + jnp.dot(a_ref[...], w_ref[...], preferred_element_type=F32)


def _down(a, w, res):
    m, k = a.shape
    n = w.shape[1]
    tm, tn = _tile(m, 512), _tile(n, 512)
    return pl.pallas_call(
        _down_kernel,
        out_shape=jax.ShapeDtypeStruct((m, n), F32),
        grid=(m // tm, n // tn),
        in_specs=[pl.BlockSpec((tm, k), lambda i, j: (i, 0)),
                  pl.BlockSpec((k, tn), lambda i, j: (0, j)),
                  pl.BlockSpec((tm, tn), lambda i, j: (i, j))],
        out_specs=pl.BlockSpec((tm, tn), lambda i, j: (i, j)),
        compiler_params=_params("parallel", "parallel"),
        name="down_proj",
    )(a, w, res)


def kernel(x, positions, mix_norm_g, w_in, q_norm_g, w_uq, kv_norm_g, w_ukv, attn_out_norm_g,
           lb_logits, hg_norm_g, w_out, ffn_norm_g, w_up, conv_w, conv_b, w_down, final_norm_g):
    batch, seq, d = x.shape
    assert w_in.shape[0] == 1 and lb_logits.shape[0] == 2, "single-layer block expected"
    m = batch * seq
    x2 = x.reshape(m, d)
    half = QK_ROPE // 2

    w = w_in[0]
    o1, o2, o3 = Q_LORA, Q_LORA + KV_LORA, Q_LORA + KV_LORA + QK_ROPE
    wkr = w[:, o2:o3]
    wkr_sw = jnp.concatenate([wkr[:, half:], wkr[:, :half]], axis=1)
    w_lat = jnp.concatenate(
        [w[:, :o2], wkr, wkr_sw, jnp.zeros((d, LATENT_PAD - o3 - QK_ROPE), w.dtype)],
        axis=1).astype(BF16)
    w_hg = w[:, o3:].astype(BF16)

    wq = w_uq[0].reshape(Q_LORA, MLA_HEADS, QK_NOPE + QK_ROPE)
    wq_rope = wq[..., QK_NOPE:]
    wq_ext = jnp.concatenate(
        [wq, wq_rope[..., half:], wq_rope[..., :half]], axis=-1
    ).reshape(Q_LORA, MLA_HEADS * Q_HEAD_W).astype(BF16)
    wkv = w_ukv[0].astype(BF16)
    mla_w = MLA_HEADS * V_DIM
    w_out1 = w_out[0, :mla_w].astype(BF16)
    w_out2 = w_out[0, mla_w:].astype(BF16)
    w_up_b = w_up[0].astype(BF16)
    w_down_b = w_down[0].astype(BF16)

    inv_freq = 1.0 / (ROPE_THETA ** (jnp.arange(half, dtype=F32) * 2.0 / QK_ROPE))
    invf = jnp.tile(inv_freq, LANE // half).reshape(1, LANE)
    pos128 = jnp.broadcast_to(positions.reshape(m, 1), (m, LANE))

    hn = _rmsnorm(x2, mix_norm_g[0], BF16, "mix_norm")
    proj_lat = _matmul(hn, w_lat, 1024, LATENT_PAD, "in_proj_latent")
    proj_hg = _matmul(hn, w_hg, 2048, 512, "in_proj_hgrn")
    q, kv, kr = _latent_up(proj_lat, pos128, invf, q_norm_g[0].reshape(1, -1),
                           kv_norm_g[0].reshape(1, -1), wq_ext, wkv)
    mla_out = _attention(q, kv, kr, attn_out_norm_g[0].reshape(1, -1), batch, seq)
    hg_out = _hgrn(proj_hg, lb_logits, hg_norm_g[0].reshape(1, -1), batch, seq)
    h1 = _outproj(mla_out, hg_out, w_out1, w_out2, x2)

    hn2 = _rmsnorm(h1, ffn_norm_g[0], BF16, "ffn_norm")
    act = _upglu(hn2, w_up_b, conv_w[0], conv_b[0].reshape(1, -1), seq)
    h2 = _down(act, w_down_b, h1)
    out = _rmsnorm(h2, final_norm_g, F32, "final_norm")
    return out.reshape(batch, seq, d)
```

```python
import functools
import math

import jax
import jax.numpy as jnp
from jax import lax
from jax.experimental import pallas as pl
from jax.experimental.pallas import tpu as pltpu

F32 = jnp.float32
BF16 = jnp.bfloat16

MLA_HEADS = 16
QK_NOPE = 128
QK_ROPE = 64
V_DIM = 128
Q_LORA = 768
KV_LORA = 512
ROPE_THETA = 10000.0
HG_HEADS = 16
HG_DIM = 128
HG_CHUNK = 64
HG_SUB = 16
CONV_WIDTH = 3
EPS = 1e-6

LANE = 128
SUBLANE = 8
VMEM_LIMIT_BYTES = 60 * 1024 * 1024

LATENT_PAD = 1536
Q_HEAD_W = 2 * LANE
LOG2_E = math.log2(math.e)
QK_LOG2_SCALE = (QK_NOPE + QK_ROPE) ** -0.5 * LOG2_E

_NT = (((1,), (1,)), ((), ()))
_TN = (((0,), (0,)), ((), ()))


def _params(*sem):
    return pltpu.CompilerParams(dimension_semantics=sem, vmem_limit_bytes=VMEM_LIMIT_BYTES)


def _tile(n, pref):
    t = min(n, pref)
    assert n % t == 0, (n, t)
    return t


def _rms_kernel(x_ref, g_ref, o_ref):
    x = x_ref[...]
    ms = jnp.mean(x * x, axis=-1, keepdims=True)
    o_ref[...] = (x * lax.rsqrt(ms + EPS) * g_ref[...]).astype(o_ref.dtype)


def _rmsnorm(x, g, out_dtype, name):
    m, d = x.shape
    tm = _tile(m, 256)
    return pl.pallas_call(
        _rms_kernel,
        out_shape=jax.ShapeDtypeStruct((m, d), out_dtype),
        grid=(m // tm,),
        in_specs=[pl.BlockSpec((tm, d), lambda i: (i, 0)),
                  pl.BlockSpec((1, d), lambda i: (0, 0))],
        out_specs=pl.BlockSpec((tm, d), lambda i: (i, 0)),
        compiler_params=_params("parallel"),
        name=name,
    )(x, g.reshape(1, d))


def _mm_kernel(a_ref, b_ref, o_ref):
    o_ref[...] = jnp.dot(a_ref[...], b_ref[...], preferred_element_type=F32).astype(o_ref.dtype)


def _matmul(a, b, tm, tn, name):
    m, k = a.shape
    _, n = b.shape
    tm, tn = _tile(m, tm), _tile(n, tn)
    return pl.pallas_call(
        _mm_kernel,
        out_shape=jax.ShapeDtypeStruct((m, n), F32),
        grid=(m // tm, n // tn),
        in_specs=[pl.BlockSpec((tm, k), lambda i, j: (i, 0)),
                  pl.BlockSpec((k, tn), lambda i, j: (0, j))],
        out_specs=pl.BlockSpec((tm, tn), lambda i, j: (i, j)),
        compiler_params=_params("parallel", "parallel"),
        name=name,
    )(a, b)


def _latent_kernel(c_ref, pos_ref, invf_ref, qg_ref, kvg_ref, wq_ref, wkv_ref,
                   q_ref, kv_ref, kr_ref):
    c = c_ref[...]
    ang = pos_ref[...].astype(F32) * invf_ref[...]
    lane = lax.broadcasted_iota(jnp.int32, ang.shape, 1)
    sin = jnp.sin(ang)
    cs = jnp.where(lane < QK_ROPE, jnp.cos(ang),
                   jnp.where(lane < QK_ROPE + QK_ROPE // 2, -sin, sin))

    cq = c[:, :Q_LORA]
    cqn = (cq * lax.rsqrt(jnp.mean(cq * cq, axis=-1, keepdims=True) + EPS)
           * qg_ref[...]).astype(BF16)
    ckv = c[:, Q_LORA:Q_LORA + KV_LORA]
    ckvn = (ckv * lax.rsqrt(jnp.mean(ckv * ckv, axis=-1, keepdims=True) + EPS)
            * kvg_ref[...]).astype(BF16)

    for h in range(MLA_HEADS):
        qh = jnp.dot(cqn, wq_ref[:, h * Q_HEAD_W:(h + 1) * Q_HEAD_W],
                     preferred_element_type=F32) * QK_LOG2_SCALE
        q_ref[:, h * Q_HEAD_W:h * Q_HEAD_W + QK_NOPE] = qh[:, :QK_NOPE].astype(BF16)
        q_ref[:, h * Q_HEAD_W + QK_NOPE:(h + 1) * Q_HEAD_W] = (qh[:, QK_NOPE:] * cs).astype(BF16)

    kv_ref[...] = jnp.dot(ckvn, wkv_ref[...], preferred_element_type=F32).astype(BF16)

    t = c[:, Q_LORA + KV_LORA:Q_LORA + KV_LORA + LANE] * cs
    kr_ref[...] = (t + pltpu.roll(t, QK_ROPE, axis=1)).astype(BF16)


def _latent_up(proj, pos128, invf, qg, kvg, wq_ext, wkv):
    m = proj.shape[0]
    ts = _tile(m, 512)
    nq = wq_ext.shape[1]
    nkv = wkv.shape[1]
    return pl.pallas_call(
        _latent_kernel,
        out_shape=(jax.ShapeDtypeStruct((m, nq), BF16),
                   jax.ShapeDtypeStruct((m, nkv), BF16),
                   jax.ShapeDtypeStruct((m, LANE), BF16)),
        grid=(m // ts,),
        in_specs=[pl.BlockSpec((ts, LATENT_PAD), lambda i: (i, 0)),
                  pl.BlockSpec((ts, LANE), lambda i: (i, 0)),
                  pl.BlockSpec((1, LANE), lambda i: (0, 0)),
                  pl.BlockSpec((1, Q_LORA), lambda i: (0, 0)),
                  pl.BlockSpec((1, KV_LORA), lambda i: (0, 0)),
                  pl.BlockSpec((Q_LORA, nq), lambda i: (0, 0)),
                  pl.BlockSpec((KV_LORA, nkv), lambda i: (0, 0))],
        out_specs=(pl.BlockSpec((ts, nq), lambda i: (i, 0)),
                   pl.BlockSpec((ts, nkv), lambda i: (i, 0)),
                   pl.BlockSpec((ts, LANE), lambda i: (i, 0))),
        compiler_params=_params("parallel"),
        name="latent_up",
    )(proj, pos128, invf, qg, kvg, wq_ext, wkv)


def _attn_kernel(q_ref, kn_ref, v_ref, kr_ref, g_ref, o_ref, kfull_ref, *, tq, tk):
    qi = pl.program_id(2)

    @pl.when(qi == 0)
    def _():
        kfull_ref[:, :QK_NOPE] = kn_ref[...]
        kfull_ref[:, QK_NOPE:] = kr_ref[...]

    nsub = tq // tk
    qs = [q_ref[j * tk:(j + 1) * tk, :] for j in range(nsub)]

    def update(q, k, v, carry, masked):
        m, l, acc = carry
        s = lax.dot_general(q, k, _NT, preferred_element_type=F32)
        if masked:
            row = lax.broadcasted_iota(jnp.int32, s.shape, 0)
            col = lax.broadcasted_iota(jnp.int32, s.shape, 1)
            s = jnp.where(col <= row, s, -jnp.inf)
        m_new = jnp.maximum(m, jnp.max(s, axis=-1, keepdims=True))
        p = jnp.exp2(s - m_new)
        alpha = jnp.exp2(m - m_new)
        l = alpha * l + jnp.sum(p, axis=-1, keepdims=True)
        acc = alpha * acc + jnp.dot(p.astype(BF16), v, preferred_element_type=F32)
        return m_new, l, acc

    def load_kv(kb):
        start = pl.multiple_of(kb * tk, tk)
        return kfull_ref[pl.ds(start, tk), :], v_ref[pl.ds(start, tk), :]

    def body(kb, carries):
        k, v = load_kv(kb)
        return tuple(update(qs[j], k, v, carries[j], False) for j in range(nsub))

    init = tuple((jnp.full((tk, 1), -jnp.inf, F32), jnp.zeros((tk, 1), F32),
                  jnp.zeros((tk, V_DIM), F32)) for _ in range(nsub))
    carries = list(lax.fori_loop(0, qi * nsub, body, init))
    for jb in range(nsub):
        k, v = load_kv(qi * nsub + jb)
        for j in range(jb, nsub):
            carries[j] = update(qs[j], k, v, carries[j], j == jb)

    for j in range(nsub):
        _, l, acc = carries[j]
        o = acc / l
        ms = jnp.mean(o * o, axis=-1, keepdims=True)
        o_ref[j * tk:(j + 1) * tk, :] = (o * lax.rsqrt(ms + EPS) * g_ref[...]).astype(o_ref.dtype)


def _attention(q, kv, kr, g, batch, seq):
    m = q.shape[0]
    tk = _tile(seq, 512)
    tq = _tile(seq, 4 * tk)
    nq = seq // tq
    return pl.pallas_call(
        functools.partial(_attn_kernel, tq=tq, tk=tk),
        out_shape=jax.ShapeDtypeStruct((m, MLA_HEADS * V_DIM), BF16),
        grid=(batch, MLA_HEADS, nq),
        in_specs=[pl.BlockSpec((tq, Q_HEAD_W), lambda b, h, i: (b * nq + i, h)),
                  pl.BlockSpec((seq, QK_NOPE), lambda b, h, i: (b, 2 * h)),
                  pl.BlockSpec((seq, V_DIM), lambda b, h, i: (b, 2 * h + 1)),
                  pl.BlockSpec((seq, LANE), lambda b, h, i: (b, 0)),
                  pl.BlockSpec((1, V_DIM), lambda b, h, i: (0, h))],
        out_specs=pl.BlockSpec((tq, V_DIM), lambda b, h, i: (b * nq + i, h)),
        scratch_shapes=[pltpu.VMEM((seq, Q_HEAD_W), BF16)],
        compiler_params=_params("parallel", "parallel", "arbitrary"),
        name="mla_attention",
    )(q, kv, kv, kr, g)


def _hgrn_chunk(z, hq, hi, gate, st, a_ref, log_lb, log1m_lb, gn, tri_ones, causal):
    sp = jnp.log(1.0 + jnp.exp(-jnp.abs(z)))
    log_sig = jnp.minimum(z, 0.0) - sp
    log_sig_neg = jnp.minimum(-z, 0.0) - sp
    b = log1m_lb + log_sig
    log_f = jnp.maximum(log_lb, b) + jnp.log(1.0 + jnp.exp(-jnp.abs(log_lb - b)))
    kk = jnp.exp(log1m_lb + log_sig_neg)

    g = log_f * LOG2_E
    g1 = g.astype(BF16)
    r1 = g - g1.astype(F32)
    g2 = r1.astype(BF16)
    g3 = (r1 - g2.astype(F32)).astype(BF16)
    gs = jnp.dot(tri_ones, jnp.concatenate([g1, g2, g3], axis=1), preferred_element_type=F32)
    G = gs[:, :LANE] + gs[:, LANE:2 * LANE] + gs[:, 2 * LANE:]

    qs = hq * (HG_DIM ** -0.5)
    vb = hi.astype(BF16)
    inter = lax.dot_general((qs * jnp.exp2(G)).astype(BF16), st.astype(BF16), _NT,
                            preferred_element_type=F32)

    for i in range(HG_CHUNK // HG_SUB):
        lo = i * HG_SUB
        Gi = G[lo:lo + HG_SUB]
        qi = qs[lo:lo + HG_SUB]
        ki = kk[lo:lo + HG_SUB]
        if i > 0:
            bnd = G[lo - 1:lo]
            qh = (qi * jnp.exp2(Gi - bnd)).astype(BF16)
            kt = (kk[:lo] * jnp.exp2(bnd - G[:lo])).astype(BF16)
            a_ref[lo:lo + HG_SUB, 0:lo] = lax.dot_general(qh, kt, _NT, preferred_element_type=F32)
        for s in range(HG_SUB):
            e = jnp.exp2(Gi - Gi[s:s + 1])
            a_ref[lo:lo + HG_SUB, lo + s:lo + s + 1] = jnp.sum(qi * ki[s:s + 1] * e, axis=-1,
                                                               keepdims=True)
    a = jnp.where(causal, a_ref[...], 0.0).astype(BF16)
    o = inter + jnp.dot(a, vb, preferred_element_type=F32)

    g_last = G[HG_CHUNK - 1:HG_CHUNK]
    kdec = (kk * jnp.exp2(g_last - G)).astype(BF16)
    st_new = st * jnp.exp2(g_last) + lax.dot_general(vb, kdec, _TN, preferred_element_type=F32)

    ms = jnp.mean(o * o, axis=-1, keepdims=True)
    y = o * lax.rsqrt(ms + EPS) * gn
    return y * (gate * jax.nn.sigmoid(gate)), st_new


def _hgrn_kernel(hq_ref, hf_ref, hi_ref, hg_ref, lb_ref, g_ref, o_ref, st_ref, a_ref, *, ts, hpb):
    si = pl.program_id(2)

    @pl.when(si == 0)
    def _():
        st_ref[...] = jnp.zeros_like(st_ref)

    a_ref[...] = jnp.zeros_like(a_ref)

    l0 = lb_ref[0:1, :]
    l1 = lb_ref[1:2, :]
    lmax = jnp.maximum(l0, l1)
    e0 = jnp.exp(l0 - lmax)
    e1 = jnp.exp(l1 - lmax)
    lb = e0 / (e0 + e1)
    log_lb = jnp.log(lb)
    log1m_lb = jnp.log1p(-lb)
    gn = g_ref[...]

    causal = (lax.broadcasted_iota(jnp.int32, (HG_CHUNK, HG_CHUNK), 0)
              >= lax.broadcasted_iota(jnp.int32, (HG_CHUNK, HG_CHUNK), 1))
    tri_ones = jnp.where(causal, 1.0, 0.0).astype(BF16)

    def chunk_body(ci, sts):
        rows = pl.ds(pl.multiple_of(ci * HG_CHUNK, HG_CHUNK), HG_CHUNK)
        new = []
        for j in range(hpb):
            cols = slice(j * HG_DIM, (j + 1) * HG_DIM)
            y, st = _hgrn_chunk(hf_ref[rows, cols], hq_ref[rows, cols], hi_ref[rows, cols],
                                hg_ref[rows, cols], sts[j], a_ref.at[j],
                                log_lb[:, cols], log1m_lb[:, cols], gn[:, cols], tri_ones, causal)
            o_ref[rows, cols] = y.astype(o_ref.dtype)
            new.append(st)
        return tuple(new)

    sts = lax.fori_loop(0, ts // HG_CHUNK, chunk_body, tuple(st_ref[j] for j in range(hpb)))
    for j in range(hpb):
        st_ref[j] = sts[j]


def _hgrn(proj, lb_logits, g, batch, seq):
    m = proj.shape[0]
    ts = _tile(seq, 512)
    ns = seq // ts
    hpb = 8
    bw = hpb * HG_DIM
    assert HG_HEADS % hpb == 0
    nb = HG_HEADS // hpb

    def spec(j):
        return pl.BlockSpec((ts, bw), lambda b, h, i: (b * ns + i, j * nb + h))

    return pl.pallas_call(
        functools.partial(_hgrn_kernel, ts=ts, hpb=hpb),
        out_shape=jax.ShapeDtypeStruct((m, HG_HEADS * HG_DIM), BF16),
        grid=(batch, nb, ns),
        in_specs=[spec(0), spec(1), spec(2), spec(3),
                  pl.BlockSpec((2, bw), lambda b, h, i: (0, h)),
                  pl.BlockSpec((1, bw), lambda b, h, i: (0, h))],
        out_specs=pl.BlockSpec((ts, bw), lambda b, h, i: (b * ns + i, h)),
        scratch_shapes=[pltpu.VMEM((hpb, HG_DIM, HG_DIM), F32),
                        pltpu.VMEM((hpb, HG_CHUNK, HG_CHUNK), F32)],
        compiler_params=_params("parallel", "parallel", "arbitrary"),
        name="hgrn2",
    )(proj, proj, proj, proj, lb_logits, g)


def _outproj_kernel(a1_ref, a2_ref, w1_ref, w2_ref, r_ref, o_ref):
    acc = jnp.dot(a1_ref[...], w1_ref[...], preferred_element_type=F32)
    acc = acc + jnp.dot(a2_ref[...], w2_ref[...], preferred_element_type=F32)
    o_ref[...] = r_ref[...] + acc


def _outproj(a1, a2, w1, w2, res):
    m, k1 = a1.shape
    k2 = a2.shape[1]
    n = w1.shape[1]
    tm, tn = _tile(m, 1024), _tile(n, 1024)
    return pl.pallas_call(
        _outproj_kernel,
        out_shape=jax.ShapeDtypeStruct((m, n), F32),
        grid=(m // tm, n // tn),
        in_specs=[pl.BlockSpec((tm, k1), lambda i, j: (i, 0)),
                  pl.BlockSpec((tm, k2), lambda i, j: (i, 0)),
                  pl.BlockSpec((k1, tn), lambda i, j: (0, j)),
                  pl.BlockSpec((k2, tn), lambda i, j: (0, j)),
                  pl.BlockSpec((tm, tn), lambda i, j: (i, j))],
        out_specs=pl.BlockSpec((tm, tn), lambda i, j: (i, j)),
        compiler_params=_params("parallel", "parallel"),
        name="out_proj",
    )(a1, a2, w1, w2, res)


def _upglu_kernel(x_ref, wg_ref, wu_ref, cw_ref, cb_ref, o_ref, halo_ref, *, tm, tiles_per_seq):
    mi = pl.program_id(0)
    n = pl.program_id(1)
    x = x_ref[...]
    gate = jnp.dot(x, wg_ref[...], preferred_element_type=F32)
    up = jnp.dot(x, wu_ref[...], preferred_element_type=F32)

    prev = halo_ref[n]
    prev = jnp.where(mi % tiles_per_seq == 0, 0.0, prev)
    p1 = prev[SUBLANE - 1:SUBLANE]
    p2 = prev[SUBLANE - 2:SUBLANE - 1]
    row = lax.broadcasted_iota(jnp.int32, gate.shape, 0)
    g1 = jnp.where(row == 0, p1, pltpu.roll(gate, 1, axis=0))
    g2 = jnp.where(row == 0, p2, jnp.where(row == 1, p1, pltpu.roll(gate, 2, axis=0)))
    cw = cw_ref[...]
    conv = cb_ref[...] + cw[0:1] * g2
    conv = conv + cw[1:2] * g1
    conv = conv + cw[2:3] * gate
    o_ref[...] = (conv * jax.nn.sigmoid(conv) * up).astype(o_ref.dtype)
    halo_ref[n] = gate[tm - SUBLANE:tm]


def _upglu(x, w_up, conv_w, conv_b, seq):
    m, k = x.shape
    dff = w_up.shape[1] // 2
    tm = _tile(seq, 2048)
    tn = 2 * LANE
    assert dff % tn == 0
    nn = dff // tn
    return pl.pallas_call(
        functools.partial(_upglu_kernel, tm=tm, tiles_per_seq=seq // tm),
        out_shape=jax.ShapeDtypeStruct((m, dff), BF16),
        grid=(m // tm, nn),
        in_specs=[pl.BlockSpec((tm, k), lambda i, j: (i, 0)),
                  pl.BlockSpec((k, tn), lambda i, j: (0, j)),
                  pl.BlockSpec((k, tn), lambda i, j: (0, nn + j)),
                  pl.BlockSpec((CONV_WIDTH, tn), lambda i, j: (0, j)),
                  pl.BlockSpec((1, tn), lambda i, j: (0, j))],
        out_specs=pl.BlockSpec((tm, tn), lambda i, j: (i, j)),
        scratch_shapes=[pltpu.VMEM((nn, SUBLANE, tn), F32)],
        compiler_params=_params("arbitrary", "arbitrary"),
        name="up_convglu",
    )(x, w_up, w_up, conv_w, conv_b)


def _down_kernel(a_ref, w_ref, r_ref, o_ref):
    o_ref[...] = r_ref[...] + jnp.dot(a_ref[...], w_ref[...], preferred_element_type=F32)


def _down(a, w, res):
    m, k = a.shape
    n = w.shape[1]
    tm, tn = _tile(m, 512), _tile(n, 512)
    return pl.pallas_call(
        _down_kernel,
        out_shape=jax.ShapeDtypeStruct((m, n), F32),
        grid=(m // tm, n // tn),
        in_specs=[pl.BlockSpec((tm, k), lambda i, j: (i, 0)),
                  pl.BlockSpec((k, tn), lambda i, j: (0, j)),
                  pl.BlockSpec((tm, tn), lambda i, j: (i, j))],
        out_specs=pl.BlockSpec((tm, tn), lambda i, j: (i, j)),
        compiler_params=_params("parallel", "parallel"),
        name="down_proj",
    )(a, w, res)


def kernel(x, positions, mix_norm_g, w_in, q_norm_g, w_uq, kv_norm_g, w_ukv, attn_out_norm_g,
           lb_logits, hg_norm_g, w_out, ffn_norm_g, w_up, conv_w, conv_b, w_down, final_norm_g):
    batch, seq, d = x.shape
    assert w_in.shape[0] == 1 and lb_logits.shape[0] == 2, "single-layer block expected"
    m = batch * seq
    x2 = x.reshape(m, d)
    half = QK_ROPE // 2

    w = w_in[0].astype(BF16)
    o2, o3 = Q_LORA + KV_LORA, Q_LORA + KV_LORA + QK_ROPE
    wkr = w[:, o2:o3]
    wkr_sw = jnp.concatenate([wkr[:, half:], wkr[:, :half]], axis=1)
    w_lat = jnp.concatenate(
        [w[:, :o2], wkr, wkr_sw, jnp.zeros((d, LATENT_PAD - o3 - QK_ROPE), w.dtype)],
        axis=1)
    w_hg = w[:, o3:]

    wq = w_uq[0].reshape(Q_LORA, MLA_HEADS, QK_NOPE + QK_ROPE)
    wq_rope = wq[..., QK_NOPE:]
    wq_ext = jnp.concatenate(
        [wq, wq_rope[..., half:], wq_rope[..., :half]], axis=-1
    ).reshape(Q_LORA, MLA_HEADS * Q_HEAD_W).astype(BF16)
    wkv = w_ukv[0].astype(BF16)
    mla_w = MLA_HEADS * V_DIM
    w_out1 = w_out[0, :mla_w].astype(BF16)
    w_out2 = w_out[0, mla_w:].astype(BF16)
    w_up_b = w_up[0].astype(BF16)
    w_down_b = w_down[0].astype(BF16)

    inv_freq = 1.0 / (ROPE_THETA ** (jnp.arange(half, dtype=F32) * 2.0 / QK_ROPE))
    invf = jnp.tile(inv_freq, LANE // half).reshape(1, LANE)
    pos128 = jnp.broadcast_to(positions.reshape(m, 1), (m, LANE))

    hn = _rmsnorm(x2, mix_norm_g[0], BF16, "mix_norm")
    proj_lat = _matmul(hn, w_lat, 1024, LATENT_PAD, "in_proj_latent")
    proj_hg = _matmul(hn, w_hg, 2048, 512, "in_proj_hgrn")
    q, kv, kr = _latent_up(proj_lat, pos128, invf, q_norm_g[0].reshape(1, -1),
                           kv_norm_g[0].reshape(1, -1), wq_ext, wkv)
    mla_out = _attention(q, kv, kr, attn_out_norm_g[0].reshape(1, -1), batch, seq)
    hg_out = _hgrn(proj_hg, lb_logits, hg_norm_g[0].reshape(1, -1), batch, seq)
    h1 = _outproj(mla_out, hg_out, w_out1, w_out2, x2)

    hn2 = _rmsnorm(h1, ffn_norm_g[0], BF16, "ffn_norm")
    act = _upglu(hn2, w_up_b, conv_w[0], conv_b[0].reshape(1, -1), seq)
    h2 = _down(act, w_down_b, h1)
    out = _rmsnorm(h2, final_norm_g, F32, "final_norm")
    return out.reshape(batch, seq, d)
```

```python
import functools
import math

import jax
import jax.numpy as jnp
from jax import lax
from jax.experimental import pallas as pl
from jax.experimental.pallas import tpu as pltpu

F32 = jnp.float32
BF16 = jnp.bfloat16

MLA_HEADS = 16
QK_NOPE = 128
QK_ROPE = 64
V_DIM = 128
Q_LORA = 768
KV_LORA = 512
ROPE_THETA = 10000.0
HG_HEADS = 16
HG_DIM = 128
HG_CHUNK = 64
HG_SUB = 16
CONV_WIDTH = 3
EPS = 1e-6

LANE = 128
SUBLANE = 8
VMEM_LIMIT_BYTES = 60 * 1024 * 1024

LATENT_PAD = 1536
Q_HEAD_W = 2 * LANE
LOG2_E = math.log2(math.e)
QK_LOG2_SCALE = (QK_NOPE + QK_ROPE) ** -0.5 * LOG2_E

_NT = (((1,), (1,)), ((), ()))
_TN = (((0,), (0,)), ((), ()))


def _params(*sem):
    return pltpu.CompilerParams(dimension_semantics=sem, vmem_limit_bytes=VMEM_LIMIT_BYTES)


def _tile(n, pref):
    t = min(n, pref)
    assert n % t == 0, (n, t)
    return t


def _rms_kernel(x_ref, g_ref, o_ref):
    x = x_ref[...]
    ms = jnp.mean(x * x, axis=-1, keepdims=True)
    o_ref[...] = (x * lax.rsqrt(ms + EPS) * g_ref[...]).astype(o_ref.dtype)


def _rmsnorm(x, g, out_dtype, name):
    m, d = x.shape
    tm = _tile(m, 256)
    return pl.pallas_call(
        _rms_kernel,
        out_shape=jax.ShapeDtypeStruct((m, d), out_dtype),
        grid=(m // tm,),
        in_specs=[pl.BlockSpec((tm, d), lambda i: (i, 0)),
                  pl.BlockSpec((1, d), lambda i: (0, 0))],
        out_specs=pl.BlockSpec((tm, d), lambda i: (i, 0)),
        compiler_params=_params("parallel"),
        name=name,
    )(x, g.reshape(1, d))


def _mm_kernel(a_ref, b_ref, o_ref):
    o_ref[...] = jnp.dot(a_ref[...], b_ref[...], preferred_element_type=F32).astype(o_ref.dtype)


def _matmul(a, b, tm, tn, name):
    m, k = a.shape
    _, n = b.shape
    tm, tn = _tile(m, tm), _tile(n, tn)
    return pl.pallas_call(
        _mm_kernel,
        out_shape=jax.ShapeDtypeStruct((m, n), F32),
        grid=(m // tm, n // tn),
        in_specs=[pl.BlockSpec((tm, k), lambda i, j: (i, 0)),
                  pl.BlockSpec((k, tn), lambda i, j: (0, j))],
        out_specs=pl.BlockSpec((tm, tn), lambda i, j: (i, j)),
        compiler_params=_params("parallel", "parallel"),
        name=name,
    )(a, b)


def _latent_kernel(c_ref, pos_ref, invf_ref, qg_ref, kvg_ref, wq_ref, wkv_ref,
                   q_ref, kv_ref, kr_ref):
    c = c_ref[...]
    ang = pos_ref[...].astype(F32) * invf_ref[...]
    lane = lax.broadcasted_iota(jnp.int32, ang.shape, 1)
    sin = jnp.sin(ang)
    cs = jnp.where(lane < QK_ROPE, jnp.cos(ang),
                   jnp.where(lane < QK_ROPE + QK_ROPE // 2, -sin, sin))

    cq = c[:, :Q_LORA]
    cqn = (cq * lax.rsqrt(jnp.mean(cq * cq, axis=-1, keepdims=True) + EPS)
           * qg_ref[...]).astype(BF16)
    ckv = c[:, Q_LORA:Q_LORA + KV_LORA]
    ckvn = (ckv * lax.rsqrt(jnp.mean(ckv * ckv, axis=-1, keepdims=True) + EPS)
            * kvg_ref[...]).astype(BF16)

    for h in range(MLA_HEADS):
        qh = jnp.dot(cqn, wq_ref[:, h * Q_HEAD_W:(h + 1) * Q_HEAD_W],
                     preferred_element_type=F32) * QK_LOG2_SCALE
        q_ref[:, h * Q_HEAD_W:h * Q_HEAD_W + QK_NOPE] = qh[:, :QK_NOPE].astype(BF16)
        q_ref[:, h * Q_HEAD_W + QK_NOPE:(h + 1) * Q_HEAD_W] = (qh[:, QK_NOPE:] * cs).astype(BF16)

    kv_ref[...] = jnp.dot(ckvn, wkv_ref[...], preferred_element_type=F32).astype(BF16)

    t = c[:, Q_LORA + KV_LORA:Q_LORA + KV_LORA + LANE] * cs
    kr_ref[...] = (t + pltpu.roll(t, QK_ROPE, axis=1)).astype(BF16)


def _latent_up(proj, pos128, invf, qg, kvg, wq_ext, wkv):
    m = proj.shape[0]
    ts = _tile(m, 512)
    nq = wq_ext.shape[1]
    nkv = wkv.shape[1]
    return pl.pallas_call(
        _latent_kernel,
        out_shape=(jax.ShapeDtypeStruct((m, nq), BF16),
                   jax.ShapeDtypeStruct((m, nkv), BF16),
                   jax.ShapeDtypeStruct((m, LANE), BF16)),
        grid=(m // ts,),
        in_specs=[pl.BlockSpec((ts, LATENT_PAD), lambda i: (i, 0)),
                  pl.BlockSpec((ts, LANE), lambda i: (i, 0)),
                  pl.BlockSpec((1, LANE), lambda i: (0, 0)),
                  pl.BlockSpec((1, Q_LORA), lambda i: (0, 0)),
                  pl.BlockSpec((1, KV_LORA), lambda i: (0, 0)),
                  pl.BlockSpec((Q_LORA, nq), lambda i: (0, 0)),
                  pl.BlockSpec((KV_LORA, nkv), lambda i: (0, 0))],
        out_specs=(pl.BlockSpec((ts, nq), lambda i: (i, 0)),
                   pl.BlockSpec((ts, nkv), lambda i: (i, 0)),
                   pl.BlockSpec((ts, LANE), lambda i: (i, 0))),
        compiler_params=_params("parallel"),
        name="latent_up",
    )(proj, pos128, invf, qg, kvg, wq_ext, wkv)


def _attn_kernel(q_ref, kn_ref, v_ref, kr_ref, g_ref, o_ref, kfull_ref, *, tq, tk):
    qi = pl.program_id(2)

    @pl.when(qi == 0)
    def _():
        kfull_ref[:, :QK_NOPE] = kn_ref[...]
        kfull_ref[:, QK_NOPE:] = kr_ref[...]

    nsub = tq // tk
    qs = [q_ref[j * tk:(j + 1) * tk, :] for j in range(nsub)]

    def update(q, k, v, carry, masked):
        m, l, acc = carry
        s = lax.dot_general(q, k, _NT, preferred_element_type=F32)
        if masked:
            row = lax.broadcasted_iota(jnp.int32, s.shape, 0)
            col = lax.broadcasted_iota(jnp.int32, s.shape, 1)
            s = jnp.where(col <= row, s, -jnp.inf)
        m_new = jnp.maximum(m, jnp.max(s, axis=-1, keepdims=True))
        p = jnp.exp2(s - m_new)
        alpha = jnp.exp2(m - m_new)
        l = alpha * l + jnp.sum(p, axis=-1, keepdims=True)
        acc = alpha * acc + jnp.dot(p.astype(BF16), v, preferred_element_type=F32)
        return m_new, l, acc

    def load_kv(kb):
        start = pl.multiple_of(kb * tk, tk)
        return kfull_ref[pl.ds(start, tk), :], v_ref[pl.ds(start, tk), :]

    def body(kb, carries):
        k, v = load_kv(kb)
        return tuple(update(qs[j], k, v, carries[j], False) for j in range(nsub))

    init = tuple((jnp.full((tk, 1), -jnp.inf, F32), jnp.zeros((tk, 1), F32),
                  jnp.zeros((tk, V_DIM), F32)) for _ in range(nsub))
    carries = list(lax.fori_loop(0, qi * nsub, body, init))
    for jb in range(nsub):
        k, v = load_kv(qi * nsub + jb)
        for j in range(jb, nsub):
            carries[j] = update(qs[j], k, v, carries[j], j == jb)

    for j in range(nsub):
        _, l, acc = carries[j]
        o = acc / l
        ms = jnp.mean(o * o, axis=-1, keepdims=True)
        o_ref[j * tk:(j + 1) * tk, :] = (o * lax.rsqrt(ms + EPS) * g_ref[...]).astype(o_ref.dtype)


def _attention(q, kv, kr, g, batch, seq):
    m = q.shape[0]
    tk = _tile(seq, 1024)
    tq = _tile(seq, 2 * tk)
    nq = seq // tq
    return pl.pallas_call(
        functools.partial(_attn_kernel, tq=tq, tk=tk),
        out_shape=jax.ShapeDtypeStruct((m, MLA_HEADS * V_DIM), BF16),
        grid=(batch, MLA_HEADS, nq),
        in_specs=[pl.BlockSpec((tq, Q_HEAD_W), lambda b, h, i: (b * nq + i, h)),
                  pl.BlockSpec((seq, QK_NOPE), lambda b, h, i: (b, 2 * h)),
                  pl.BlockSpec((seq, V_DIM), lambda b, h, i: (b, 2 * h + 1)),
                  pl.BlockSpec((seq, LANE), lambda b, h, i: (b, 0)),
                  pl.BlockSpec((1, V_DIM), lambda b, h, i: (0, h))],
        out_specs=pl.BlockSpec((tq, V_DIM), lambda b, h, i: (b * nq + i, h)),
        scratch_shapes=[pltpu.VMEM((seq, Q_HEAD_W), BF16)],
        compiler_params=_params("parallel", "parallel", "arbitrary"),
        name="mla_attention",
    )(q, kv, kv, kr, g)


def _hgrn_chunk(z, hq, hi, gate, st, a_ref, log_lb, log1m_lb, gn, tri_ones, causal):
    sp = jnp.log(1.0 + jnp.exp(-jnp.abs(z)))
    log_sig = jnp.minimum(z, 0.0) - sp
    log_sig_neg = jnp.minimum(-z, 0.0) - sp
    b = log1m_lb + log_sig
    log_f = jnp.maximum(log_lb, b) + jnp.log(1.0 + jnp.exp(-jnp.abs(log_lb - b)))
    kk = jnp.exp(log1m_lb + log_sig_neg)

    g = log_f * LOG2_E
    g1 = g.astype(BF16)
    r1 = g - g1.astype(F32)
    g2 = r1.astype(BF16)
    g3 = (r1 - g2.astype(F32)).astype(BF16)
    gs = jnp.dot(tri_ones, jnp.concatenate([g1, g2, g3], axis=1), preferred_element_type=F32)
    G = gs[:, :LANE] + gs[:, LANE:2 * LANE] + gs[:, 2 * LANE:]

    qs = hq * (HG_DIM ** -0.5)
    vb = hi.astype(BF16)
    inter = lax.dot_general((qs * jnp.exp2(G)).astype(BF16), st.astype(BF16), _NT,
                            preferred_element_type=F32)

    for i in range(HG_CHUNK // HG_SUB):
        lo = i * HG_SUB
        Gi = G[lo:lo + HG_SUB]
        qi = qs[lo:lo + HG_SUB]
        ki = kk[lo:lo + HG_SUB]
        if i > 0:
            bnd = G[lo - 1:lo]
            qh = (qi * jnp.exp2(Gi - bnd)).astype(BF16)
            kt = (kk[:lo] * jnp.exp2(bnd - G[:lo])).astype(BF16)
            a_ref[lo:lo + HG_SUB, 0:lo] = lax.dot_general(qh, kt, _NT, preferred_element_type=F32)
        for s in range(HG_SUB):
            e = jnp.exp2(Gi - Gi[s:s + 1])
            a_ref[lo:lo + HG_SUB, lo + s:lo + s + 1] = jnp.sum(qi * ki[s:s + 1] * e, axis=-1,
                                                               keepdims=True)
    a = jnp.where(causal, a_ref[...], 0.0).astype(BF16)
    o = inter + jnp.dot(a, vb, preferred_element_type=F32)

    g_last = G[HG_CHUNK - 1:HG_CHUNK]
    kdec = (kk * jnp.exp2(g_last - G)).astype(BF16)
    st_new = st * jnp.exp2(g_last) + lax.dot_general(vb, kdec, _TN, preferred_element_type=F32)

    ms = jnp.mean(o * o, axis=-1, keepdims=True)
    y = o * lax.rsqrt(ms + EPS) * gn
    return y * (gate * jax.nn.sigmoid(gate)), st_new


def _hgrn_kernel(hq_ref, hf_ref, hi_ref, hg_ref, lb_ref, g_ref, o_ref, st_ref, a_ref, *, ts, hpb):
    si = pl.program_id(2)

    @pl.when(si == 0)
    def _():
        st_ref[...] = jnp.zeros_like(st_ref)

    a_ref[...] = jnp.zeros_like(a_ref)

    l0 = lb_ref[0:1, :]
    l1 = lb_ref[1:2, :]
    lmax = jnp.maximum(l0, l1)
    e0 = jnp.exp(l0 - lmax)
    e1 = jnp.exp(l1 - lmax)
    lb = e0 / (e0 + e1)
    log_lb = jnp.log(lb)
    log1m_lb = jnp.log1p(-lb)
    gn = g_ref[...]

    causal = (lax.broadcasted_iota(jnp.int32, (HG_CHUNK, HG_CHUNK), 0)
              >= lax.broadcasted_iota(jnp.int32, (HG_CHUNK, HG_CHUNK), 1))
    tri_ones = jnp.where(causal, 1.0, 0.0).astype(BF16)

    def chunk_body(ci, sts):
        rows = pl.ds(pl.multiple_of(ci * HG_CHUNK, HG_CHUNK), HG_CHUNK)
        new = []
        for j in range(hpb):
            cols = slice(j * HG_DIM, (j + 1) * HG_DIM)
            y, st = _hgrn_chunk(hf_ref[rows, cols], hq_ref[rows, cols], hi_ref[rows, cols],
                                hg_ref[rows, cols], sts[j], a_ref.at[j],
                                log_lb[:, cols], log1m_lb[:, cols], gn[:, cols], tri_ones, causal)
            o_ref[rows, cols] = y.astype(o_ref.dtype)
            new.append(st)
        return tuple(new)

    sts = lax.fori_loop(0, ts // HG_CHUNK, chunk_body, tuple(st_ref[j] for j in range(hpb)))
    for j in range(hpb):
        st_ref[j] = sts[j]


def _hgrn(proj, lb_logits, g, batch, seq):
    m = proj.shape[0]
    ts = _tile(seq, 512)
    ns = seq // ts
    hpb = 16
    bw = hpb * HG_DIM
    assert HG_HEADS % hpb == 0
    nb = HG_HEADS // hpb

    def spec(j):
        return pl.BlockSpec((ts, bw), lambda b, h, i: (b * ns + i, j * nb + h))

    return pl.pallas_call(
        functools.partial(_hgrn_kernel, ts=ts, hpb=hpb),
        out_shape=jax.ShapeDtypeStruct((m, HG_HEADS * HG_DIM), BF16),
        grid=(batch, nb, ns),
        in_specs=[spec(0), spec(1), spec(2), spec(3),
                  pl.BlockSpec((2, bw), lambda b, h, i: (0, h)),
                  pl.BlockSpec((1, bw), lambda b, h, i: (0, h))],
        out_specs=pl.BlockSpec((ts, bw), lambda b, h, i: (b * ns + i, h)),
        scratch_shapes=[pltpu.VMEM((hpb, HG_DIM, HG_DIM), F32),
                        pltpu.VMEM((hpb, HG_CHUNK, HG_CHUNK), F32)],
        compiler_params=_params("parallel", "parallel", "arbitrary"),
        name="hgrn2",
    )(proj, proj, proj, proj, lb_logits, g)


def _outproj_kernel(a1_ref, a2_ref, w1_ref, w2_ref, r_ref, o_ref):
    acc = jnp.dot(a1_ref[...], w1_ref[...], preferred_element_type=F32)
    acc = acc + jnp.dot(a2_ref[...], w2_ref[...], preferred_element_type=F32)
    o_ref[...] = r_ref[...] + acc


def _outproj(a1, a2, w1, w2, res):
    m, k1 = a1.shape
    k2 = a2.shape[1]
    n = w1.shape[1]
    tm, tn = _tile(m, 1024), _tile(n, 1024)
    return pl.pallas_call(
        _outproj_kernel,
        out_shape=jax.ShapeDtypeStruct((m, n), F32),
        grid=(m // tm, n // tn),
        in_specs=[pl.BlockSpec((tm, k1), lambda i, j: (i, 0)),
                  pl.BlockSpec((tm, k2), lambda i, j: (i, 0)),
                  pl.BlockSpec((k1, tn), lambda i, j: (0, j)),
                  pl.BlockSpec((k2, tn), lambda i, j: (0, j)),
                  pl.BlockSpec((tm, tn), lambda i, j: (i, j))],
        out_specs=pl.BlockSpec((tm, tn), lambda i, j: (i, j)),
        compiler_params=_params("parallel", "parallel"),
        name="out_proj",
    )(a1, a2, w1, w2, res)


def _upglu_kernel(x_ref, wg_ref, wu_ref, cw_ref, cb_ref, o_ref, halo_ref, *, tm, tiles_per_seq):
    mi = pl.program_id(0)
    n = pl.program_id(1)
    x = x_ref[...]
    gate = jnp.dot(x, wg_ref[...], preferred_element_type=F32)
    up = jnp.dot(x, wu_ref[...], preferred_element_type=F32)

    prev = halo_ref[n]
    prev = jnp.where(mi % tiles_per_seq == 0, 0.0, prev)
    p1 = prev[SUBLANE - 1:SUBLANE]
    p2 = prev[SUBLANE - 2:SUBLANE - 1]
    row = lax.broadcasted_iota(jnp.int32, gate.shape, 0)
    g1 = jnp.where(row == 0, p1, pltpu.roll(gate, 1, axis=0))
    g2 = jnp.where(row == 0, p2, jnp.where(row == 1, p1, pltpu.roll(gate, 2, axis=0)))
    cw = cw_ref[...]
    conv = cb_ref[...] + cw[0:1] * g2
    conv = conv + cw[1:2] * g1
    conv = conv + cw[2:3] * gate
    o_ref[...] = (conv * jax.nn.sigmoid(conv) * up).astype(o_ref.dtype)
    halo_ref[n] = gate[tm - SUBLANE:tm]


def _upglu(x, w_up, conv_w, conv_b, seq):
    m, k = x.shape
    dff = w_up.shape[1] // 2
    tm = _tile(seq, 2048)
    tn = 2 * LANE
    assert dff % tn == 0
    nn = dff // tn
    return pl.pallas_call(
        functools.partial(_upglu_kernel, tm=tm, tiles_per_seq=seq // tm),
        out_shape=jax.ShapeDtypeStruct((m, dff), BF16),
        grid=(m // tm, nn),
        in_specs=[pl.BlockSpec((tm, k), lambda i, j: (i, 0)),
                  pl.BlockSpec((k, tn), lambda i, j: (0, j)),
                  pl.BlockSpec((k, tn), lambda i, j: (0, nn + j)),
                  pl.BlockSpec((CONV_WIDTH, tn), lambda i, j: (0, j)),
                  pl.BlockSpec((1, tn), lambda i, j: (0, j))],
        out_specs=pl.BlockSpec((tm, tn), lambda i, j: (i, j)),
        scratch_shapes=[pltpu.VMEM((nn, SUBLANE, tn), F32)],
        compiler_params=_params("arbitrary", "arbitrary"),
        name="up_convglu",
    )(x, w_up, w_up, conv_w, conv_b)


def _down_kernel(a_ref, w_ref, r_ref, o_ref):
    o_ref[...] = r_ref[...] + jnp.dot(a_ref[...], w_ref[...], preferred_element_type=F32)


def _down(a, w, res):
    m, k = a.shape
    n = w.shape[1]
    tm, tn = _tile(m, 512), _tile(n, 512)
    return pl.pallas_call(
        _down_kernel,
        out_shape=jax.ShapeDtypeStruct((m, n), F32),
        grid=(m // tm, n // tn),
        in_specs=[pl.BlockSpec((tm, k), lambda i, j: (i, 0)),
                  pl.BlockSpec((k, tn), lambda i, j: (0, j)),
                  pl.BlockSpec((tm, tn), lambda i, j: (i, j))],
        out_specs=pl.BlockSpec((tm, tn), lambda i, j: (i, j)),
        compiler_params=_params("parallel", "parallel"),
        name="down_proj",
    )(a, w, res)


def kernel(x, positions, mix_norm_g, w_in, q_norm_g, w_uq, kv_norm_g, w_ukv, attn_out_norm_g,
           lb_logits, hg_norm_g, w_out, ffn_norm_g, w_up, conv_w, conv_b, w_down, final_norm_g):
    batch, seq, d = x.shape
    assert w_in.shape[0] == 1 and lb_logits.shape[0] == 2, "single-layer block expected"
    m = batch * seq
    x2 = x.reshape(m, d)
    half = QK_ROPE // 2

    w = w_in[0].astype(BF16)
    o2, o3 = Q_LORA + KV_LORA, Q_LORA + KV_LORA + QK_ROPE
    wkr = w[:, o2:o3]
    wkr_sw = jnp.concatenate([wkr[:, half:], wkr[:, :half]], axis=1)
    w_lat = jnp.concatenate(
        [w[:, :o2], wkr, wkr_sw, jnp.zeros((d, LATENT_PAD - o3 - QK_ROPE), w.dtype)],
        axis=1)
    w_hg = w[:, o3:]

    wq = w_uq[0].reshape(Q_LORA, MLA_HEADS, QK_NOPE + QK_ROPE)
    wq_rope = wq[..., QK_NOPE:]
    wq_ext = jnp.concatenate(
        [wq, wq_rope[..., half:], wq_rope[..., :half]], axis=-1
    ).reshape(Q_LORA, MLA_HEADS * Q_HEAD_W).astype(BF16)
    wkv = w_ukv[0].astype(BF16)
    mla_w = MLA_HEADS * V_DIM
    w_out1 = w_out[0, :mla_w].astype(BF16)
    w_out2 = w_out[0, mla_w:].astype(BF16)
    w_up_b = w_up[0].astype(BF16)
    w_down_b = w_down[0].astype(BF16)

    inv_freq = 1.0 / (ROPE_THETA ** (jnp.arange(half, dtype=F32) * 2.0 / QK_ROPE))
    invf = jnp.tile(inv_freq, LANE // half).reshape(1, LANE)
    pos128 = jnp.broadcast_to(positions.reshape(m, 1), (m, LANE))

    hn = _rmsnorm(x2, mix_norm_g[0], BF16, "mix_norm")
    proj_lat = _matmul(hn, w_lat, 1024, LATENT_PAD, "in_proj_latent")
    proj_hg = _matmul(hn, w_hg, 2048, 512, "in_proj_hgrn")
    q, kv, kr = _latent_up(proj_lat, pos128, invf, q_norm_g[0].reshape(1, -1),
                           kv_norm_g[0].reshape(1, -1), wq_ext, wkv)
    mla_out = _attention(q, kv, kr, attn_out_norm_g[0].reshape(1, -1), batch, seq)
    hg_out = _hgrn(proj_hg, lb_logits, hg_norm_g[0].reshape(1, -1), batch, seq)
    h1 = _outproj(mla_out, hg_out, w_out1, w_out2, x2)

    hn2 = _rmsnorm(h1, ffn_norm_g[0], BF16, "ffn_norm")
    act = _upglu(hn2, w_up_b, conv_w[0], conv_b[0].reshape(1, -1), seq)
    h2 = _down(act, w_down_b, h1)
    out = _rmsnorm(h2, final_norm_g, F32, "final_norm")
    return out.reshape(batch, seq, d)
```
